```python
import math
import jax, jax.numpy as jnp
from jax import lax
import numpy as np

D_MODEL = 1024
BATCH = 8
SEQ = 2048
DEPTH = 4
DEC_BATCH = 128
DEC_SEQ = 4
PAST_LEN = 16384
PAGE_SIZE = 128

D_MIX = 2 * D_MODEL
POOL_WIDTH = D_MIX // 4
POOL_WINDOWS = (2, 4, 8, 16)
POOL_GROUPS = len(POOL_WINDOWS)
POOL_GDIM = POOL_WIDTH // POOL_GROUPS
POOL_BUF = max(POOL_WINDOWS) - 1
SSD_WIDTH = D_MIX // 2
SSD_HEAD_DIM = 64
SSD_HEADS = SSD_WIDTH // SSD_HEAD_DIM
SSD_GROUPS = 2
SSD_HPG = SSD_HEADS // SSD_GROUPS
SSD_STATE = 128
SSD_CHUNK = 128
SSD_GN = SSD_GROUPS * SSD_STATE
SSD_CONV_DIM = SSD_WIDTH + 2 * SSD_GN
CONV_WIDTH = 4
LRU_WIDTH = D_MIX // 4
LRU_HEADS = 8
LRU_HDIM = LRU_WIDTH // LRU_HEADS
LRU_C = 8.0
N_MEM = 256
MEM_HEADS = 4
MEM_HDIM = D_MODEL // MEM_HEADS
D_FF = -(-8 * D_MODEL // (3 * 256)) * 256
EPS = 1e-6
OFF_POOL = 0
OFF_Z = OFF_POOL + POOL_WIDTH
OFF_XBC = OFF_Z + SSD_WIDTH
OFF_DT = OFF_XBC + SSD_CONV_DIM
OFF_GATE = OFF_DT + SSD_HEADS
OFF_LRU = OFF_GATE + LRU_WIDTH
N_IN = OFF_LRU + LRU_WIDTH

kernel_name = "hybrid_pool_ssd_rglru_memxattn_step"

F32 = jnp.float32


def rmsnorm(x, g):
    xf = x.astype(F32)
    var = jnp.mean(xf * xf, axis=-1, keepdims=True)
    return (xf * lax.rsqrt(var + EPS) * g.astype(F32)).astype(x.dtype)


def causal_dwconv(x, buf, w, b):
    L = x.shape[1]
    xp = jnp.concatenate([buf.astype(x.dtype), x], axis=1)
    y = b
    for k in range(CONV_WIDTH):
        y = y + w[k] * xp[:, k:k + L]
    return y, xp[:, -(CONV_WIDTH - 1):]


def pool_mixer(u, buf, pos0, w_grp, scale):
    B, L, _ = u.shape
    up = jnp.concatenate([buf.astype(u.dtype), u], axis=1).astype(F32)
    cs = jnp.pad(jnp.cumsum(up, axis=1), ((0, 0), (1, 0), (0, 0)))
    end = cs[:, POOL_BUF + 1:]
    pos = pos0 + jnp.arange(L)
    outs = []
    for g, w in enumerate(POOL_WINDOWS):
        sl = slice(g * POOL_GDIM, (g + 1) * POOL_GDIM)
        start = cs[:, POOL_BUF + 1 - w:POOL_BUF + 1 - w + L, sl]
        cnt = jnp.minimum(pos + 1, w).astype(F32)[None, :, None]
        outs.append((end[..., sl] - start) / cnt)
    pooled = jnp.concatenate(outs, axis=-1) - up[:, POOL_BUF:]
    pooled = pooled.reshape(B, L, POOL_GROUPS, POOL_GDIM)
    y = jnp.einsum('blgc,gcd->blgd', pooled, w_grp.astype(F32)).reshape(B, L, POOL_WIDTH)
    y = y * scale.astype(F32)
    return y.astype(u.dtype), up[:, -POOL_BUF:].astype(u.dtype)


def ssd_mixer(z, xbc, dt_raw, conv_buf, h0, conv_w, conv_b, dt_bias, a_log, d_skip, norm_g):
    B, L, _ = z.shape
    G, K, P, N = SSD_GROUPS, SSD_HPG, SSD_HEAD_DIM, SSD_STATE
    xbc, new_conv = causal_dwconv(xbc, conv_buf, conv_w, conv_b)
    xbc = jax.nn.silu(xbc).astype(F32)
    xs = xbc[..., :SSD_WIDTH].reshape(B, L, G, K, P)
    Bm = xbc[..., SSD_WIDTH:SSD_WIDTH + SSD_GN].reshape(B, L, G, N)
    Cm = xbc[..., SSD_WIDTH + SSD_GN:].reshape(B, L, G, N)
    dt = jax.nn.softplus(dt_raw.astype(F32) + dt_bias.astype(F32)).reshape(B, L, G, K)
    A = -jnp.exp(a_log.astype(F32)).reshape(G, K)
    Q = min(SSD_CHUNK, L)
    nC = -(-L // Q)
    pad = nC * Q - L

    def chunk(t):
        t = jnp.pad(t, ((0, 0), (0, pad)) + ((0, 0),) * (t.ndim - 2))
        return t.reshape((B, nC, Q) + t.shape[2:])

    xs, Bm, Cm, dt = chunk(xs), chunk(Bm), chunk(Cm), chunk(dt)
    acs = jnp.cumsum(dt * A, axis=2)
    xdt = xs * dt[..., None]
    acs_t = jnp.einsum('bcqgk->bcgkq', acs)
    seg = acs_t[..., :, None] - acs_t[..., None, :]
    mask = jnp.tril(jnp.ones((Q, Q), dtype=bool))
    Lmat = jnp.where(mask, jnp.exp(jnp.where(mask, seg, 0.0)), 0.0)
    CB = jnp.einsum('bclgn,bcsgn->bcgls', Cm, Bm)
    y_diag = jnp.einsum('bcgls,bcgkls,bcsgkp->bclgkp', CB, Lmat, xdt)
    decay_st = jnp.exp(acs[:, :, -1:] - acs)
    states = jnp.einsum('bcsgn,bcsgk,bcsgkp->bcgkpn', Bm, decay_st, xdt)
    chunk_decay = jnp.exp(acs[:, :, -1])

    def step(h, inp):
        st, dec = inp
        return h * dec[..., None, None] + st, h

    h_init = h0.astype(F32).reshape(B, G, K, P, N)
    hT, starts = lax.scan(step, h_init, (jnp.moveaxis(states, 1, 0), jnp.moveaxis(chunk_decay, 1, 0)))
    starts = jnp.moveaxis(starts, 0, 1)
    y_off = jnp.einsum('bclgn,bcgkpn,bclgk->bclgkp', Cm, starts, jnp.exp(acs))
    y = y_diag + y_off + xs * d_skip.astype(F32).reshape(G, K)[..., None]
    y = y.reshape(B, nC * Q, SSD_WIDTH)[:, :L]
    y = rmsnorm(y * jax.nn.silu(z.astype(F32)), norm_g)
    return y.astype(z.dtype), new_conv, hT.reshape(B, SSD_HEADS, P, N).astype(z.dtype)


def rglru_mixer(gate_in, xr, conv_buf, h0, pos0, conv_w, conv_b, wa, ba, wx, bx, lam):
    B, L, _ = xr.shape
    xc, new_conv = causal_dwconv(xr, conv_buf, conv_w, conv_b)
    xf = xc.astype(F32)
    xh = xf.reshape(B, L, LRU_HEADS, LRU_HDIM)
    r = jax.nn.sigmoid(jnp.einsum('blhi,hij->blhj', xh, wa.astype(F32)) + ba.astype(F32)).reshape(B, L, LRU_WIDTH)
    i = jax.nn.sigmoid(jnp.einsum('blhi,hij->blhj', xh, wx.astype(F32)) + bx.astype(F32)).reshape(B, L, LRU_WIDTH)
    log_a = -LRU_C * r * jax.nn.softplus(-lam.astype(F32))
    a = jnp.exp(log_a)
    pos = pos0 + jnp.arange(L)
    mult = jnp.where((pos == 0)[None, :, None], 1.0, jnp.sqrt(-jnp.expm1(2.0 * log_a)))
    b_in = mult * i * xf
    b_in = b_in.at[:, 0].add(a[:, 0] * h0.astype(F32))

    def comb(lhs, rhs):
        a1, b1 = lhs
        a2, b2 = rhs
        return a1 * a2, a2 * b1 + b2

    _, h = lax.associative_scan(comb, (a, b_in), axis=1)
    y = h * jax.nn.gelu(gate_in.astype(F32))
    return y.astype(xr.dtype), new_conv, h[:, -1].astype(xr.dtype)


def mem_attention(u, k, v, wq, wo):
    B, L, _ = u.shape
    q = (u @ wq).reshape(B, L, MEM_HEADS, MEM_HDIM)
    s = jnp.einsum('blhd,bmhd->bhlm', q, k).astype(F32) * (MEM_HDIM ** -0.5)
    p = jax.nn.softmax(s, axis=-1).astype(v.dtype)
    o = jnp.einsum('bhlm,bmhd->blhd', p, v).reshape(B, L, D_MODEL)
    return o @ wo


def run_trunk(x, pos0, pool_s, sconv_s, ssd_s, lconv_s, lru_s, mem_k, mem_v, W):
    h = x
    n_pool, n_sconv, n_ssd, n_lconv, n_lru = [], [], [], [], []
    for l in range(DEPTH):
        u = rmsnorm(h, W['norm_mix'][l])
        proj = jnp.einsum('bld,de->ble', u, W['w_in'][l])
        ya, s1 = pool_mixer(proj[..., OFF_POOL:OFF_Z], pool_s[l], pos0, W['pool_w'][l], W['pool_scale'][l])
        yb, s2, s3 = ssd_mixer(proj[..., OFF_Z:OFF_XBC], proj[..., OFF_XBC:OFF_DT], proj[..., OFF_DT:OFF_GATE],
                               sconv_s[l], ssd_s[l], W['ssd_conv_w'][l], W['ssd_conv_b'][l], W['ssd_dt_bias'][l],
                               W['ssd_a_log'][l], W['ssd_d'][l], W['ssd_norm'][l])
        yc, s4, s5 = rglru_mixer(proj[..., OFF_GATE:OFF_LRU], proj[..., OFF_LRU:N_IN], lconv_s[l], lru_s[l], pos0,
                                 W['lru_conv_w'][l], W['lru_conv_b'][l], W['lru_wa'][l], W['lru_ba'][l],
                                 W['lru_wx'][l], W['lru_bx'][l], W['lru_lambda'][l])
        h = h + jnp.concatenate([ya, yb, yc], axis=-1) @ W['w_out'][l]
        h = h + mem_attention(rmsnorm(h, W['norm_mem'][l]), mem_k[l], mem_v[l], W['w_mem_q'][l], W['w_mem_o'][l])
        u = rmsnorm(h, W['norm_ffn'][l])
        h = h + (jax.nn.silu(u @ W['w_ffn_gate'][l]) * (u @ W['w_ffn_up'][l])) @ W['w_ffn_down'][l]
        n_pool.append(s1); n_sconv.append(s2); n_ssd.append(s3); n_lconv.append(s4); n_lru.append(s5)
    y = rmsnorm(h, W['norm_final'])
    return (y, jnp.stack(n_pool), jnp.stack(n_sconv), jnp.stack(n_ssd), jnp.stack(n_lconv), jnp.stack(n_lru))


def setup_inputs(seed: int = 0) -> dict:
    key = jax.random.key(seed)
    ks = iter(jax.random.split(key, 64))

    def nrm(shape, scale=1.0):
        return jax.random.normal(next(ks), shape, F32) * scale

    def gain(shape):
        return 1.0 + nrm(shape, 0.02)

    dt0 = jnp.exp(jax.random.uniform(next(ks), (DEPTH, SSD_HEADS), F32, math.log(1e-3), math.log(1e-1)))
    base = jax.random.uniform(next(ks), (DEPTH, LRU_WIDTH), F32, 0.9, 0.999) ** (1.0 / LRU_C)
    return {
        "x_prompt": nrm((BATCH, SEQ, D_MODEL)),
        "x_sample": nrm((DEC_BATCH, DEC_SEQ, D_MODEL)),
        "mem_prompt": nrm((BATCH, N_MEM, D_MODEL)),
        "state_pool": nrm((DEPTH, DEC_BATCH, POOL_BUF, POOL_WIDTH)),
        "state_ssd_conv": nrm((DEPTH, DEC_BATCH, CONV_WIDTH - 1, SSD_CONV_DIM)),
        "state_ssd": nrm((DEPTH, DEC_BATCH, SSD_HEADS, SSD_HEAD_DIM, SSD_STATE), 0.1),
        "state_lru_conv": nrm((DEPTH, DEC_BATCH, CONV_WIDTH - 1, LRU_WIDTH)),
        "state_lru": nrm((DEPTH, DEC_BATCH, LRU_WIDTH)),
        "cache_mem_k": nrm((DEPTH, DEC_BATCH, N_MEM, MEM_HEADS, MEM_HDIM)),
        "cache_mem_v": nrm((DEPTH, DEC_BATCH, N_MEM, MEM_HEADS, MEM_HDIM)),
        "norm_mix": gain((DEPTH, D_MODEL)),
        "w_in": nrm((DEPTH, D_MODEL, N_IN), D_MODEL ** -0.5),
        "pool_w": nrm((DEPTH, POOL_GROUPS, POOL_GDIM, POOL_GDIM), POOL_GDIM ** -0.5),
        "pool_scale": gain((DEPTH, POOL_WIDTH)),
        "ssd_conv_w": nrm((DEPTH, CONV_WIDTH, SSD_CONV_DIM), CONV_WIDTH ** -0.5),
        "ssd_conv_b": nrm((DEPTH, SSD_CONV_DIM), 0.01),
        "ssd_dt_bias": dt0 + jnp.log(-jnp.expm1(-dt0)),
        "ssd_a_log": jnp.log(jax.random.uniform(next(ks), (DEPTH, SSD_HEADS), F32, 1.0, 16.0)),
        "ssd_d": gain((DEPTH, SSD_HEADS)),
        "ssd_norm": gain((DEPTH, SSD_WIDTH)),
        "lru_conv_w": nrm((DEPTH, CONV_WIDTH, LRU_WIDTH), CONV_WIDTH ** -0.5),
        "lru_conv_b": nrm((DEPTH, LRU_WIDTH), 0.01),
        "lru_wa": nrm((DEPTH, LRU_HEADS, LRU_HDIM, LRU_HDIM), LRU_HDIM ** -0.5),
        "lru_ba": nrm((DEPTH, LRU_HEADS, LRU_HDIM), 0.01),
        "lru_wx": nrm((DEPTH, LRU_HEADS, LRU_HDIM, LRU_HDIM), LRU_HDIM ** -0.5),
        "lru_bx": nrm((DEPTH, LRU_HEADS, LRU_HDIM), 0.01),
        "lru_lambda": jnp.log(base / (1.0 - base)),
        "w_out": nrm((DEPTH, D_MIX, D_MODEL), D_MIX ** -0.5),
        "norm_mem": gain((DEPTH, D_MODEL)),
        "w_mem_q": nrm((DEPTH, D_MODEL, D_MODEL), D_MODEL ** -0.5),
        "w_mem_k": nrm((DEPTH, D_MODEL, D_MODEL), D_MODEL ** -0.5),
        "w_mem_v": nrm((DEPTH, D_MODEL, D_MODEL), D_MODEL ** -0.5),
        "w_mem_o": nrm((DEPTH, D_MODEL, D_MODEL), D_MODEL ** -0.5),
        "norm_ffn": gain((DEPTH, D_MODEL)),
        "w_ffn_gate": nrm((DEPTH, D_MODEL, D_FF), D_MODEL ** -0.5),
        "w_ffn_up": nrm((DEPTH, D_MODEL, D_FF), D_MODEL ** -0.5),
        "w_ffn_down": nrm((DEPTH, D_FF, D_MODEL), D_FF ** -0.5),
        "norm_final": gain((D_MODEL,)),
    }


def reference(x_prompt, x_sample, mem_prompt, state_pool, state_ssd_conv, state_ssd, state_lru_conv, state_lru,
              cache_mem_k, cache_mem_v, norm_mix, w_in, pool_w, pool_scale, ssd_conv_w, ssd_conv_b, ssd_dt_bias,
              ssd_a_log, ssd_d, ssd_norm, lru_conv_w, lru_conv_b, lru_wa, lru_ba, lru_wx, lru_bx, lru_lambda,
              w_out, norm_mem, w_mem_q, w_mem_k, w_mem_v, w_mem_o, norm_ffn, w_ffn_gate, w_ffn_up, w_ffn_down,
              norm_final):
    W = dict(norm_mix=norm_mix, w_in=w_in, pool_w=pool_w, pool_scale=pool_scale, ssd_conv_w=ssd_conv_w,
             ssd_conv_b=ssd_conv_b, ssd_dt_bias=ssd_dt_bias, ssd_a_log=ssd_a_log, ssd_d=ssd_d, ssd_norm=ssd_norm,
             lru_conv_w=lru_conv_w, lru_conv_b=lru_conv_b, lru_wa=lru_wa, lru_ba=lru_ba, lru_wx=lru_wx,
             lru_bx=lru_bx, lru_lambda=lru_lambda, w_out=w_out, norm_mem=norm_mem, w_mem_q=w_mem_q,
             w_mem_o=w_mem_o, norm_ffn=norm_ffn, w_ffn_gate=w_ffn_gate, w_ffn_up=w_ffn_up,
             w_ffn_down=w_ffn_down, norm_final=norm_final)
    dt = x_prompt.dtype
    p_mem_k = jnp.einsum('bmd,lde->lbme', mem_prompt, w_mem_k).reshape(DEPTH, BATCH, N_MEM, MEM_HEADS, MEM_HDIM)
    p_mem_v = jnp.einsum('bmd,lde->lbme', mem_prompt, w_mem_v).reshape(DEPTH, BATCH, N_MEM, MEM_HEADS, MEM_HDIM)
    z_pool = jnp.zeros((DEPTH, BATCH, POOL_BUF, POOL_WIDTH), dt)
    z_sconv = jnp.zeros((DEPTH, BATCH, CONV_WIDTH - 1, SSD_CONV_DIM), dt)
    z_ssd = jnp.zeros((DEPTH, BATCH, SSD_HEADS, SSD_HEAD_DIM, SSD_STATE), dt)
    z_lconv = jnp.zeros((DEPTH, BATCH, CONV_WIDTH - 1, LRU_WIDTH), dt)
    z_lru = jnp.zeros((DEPTH, BATCH, LRU_WIDTH), dt)
    y_prompt, p_pool, p_sconv, p_ssd, p_lconv, p_lru = run_trunk(
        x_prompt, 0, z_pool, z_sconv, z_ssd, z_lconv, z_lru, p_mem_k, p_mem_v, W)
    y_sample, s_pool, s_sconv, s_ssd, s_lconv, s_lru = run_trunk(
        x_sample, PAST_LEN, state_pool, state_ssd_conv, state_ssd, state_lru_conv, state_lru,
        cache_mem_k, cache_mem_v, W)
    return (y_prompt, y_sample, p_pool, p_sconv, p_ssd, p_lconv, p_lru, p_mem_k, p_mem_v,
            s_pool, s_sconv, s_ssd, s_lconv, s_lru)
```

```python
import functools
import math

import numpy as np
import jax
import jax.numpy as jnp
from jax import lax
from jax.experimental import pallas as pl
from jax.experimental.pallas import tpu as pltpu

F32 = jnp.float32
BF16 = jnp.bfloat16

D_MODEL = 1024
DEPTH = 4
EPS = 1e-6
POOL_WIDTH = 512
POOL_WINDOWS = (2, 4, 8, 16)
POOL_GDIM = 128
POOL_BUF = 15
SSD_WIDTH = 1024
SSD_HEAD_DIM = 64
SSD_HEADS = 16
SSD_GROUPS = 2
SSD_GWIDTH = SSD_WIDTH // SSD_GROUPS
SSD_STATE = 128
SSD_CONV_DIM = 1536
CONV_WIDTH = 4
LRU_WIDTH = 512
LRU_HEADS = 8
LRU_HDIM = 64
LRU_C = 8.0
N_MEM = 256
MEM_HEADS = 4
MEM_HDIM = 256
D_FF = 2816
D_MIX = 2048
PAST_LEN = 16384

C_POOL = 0
C_Z = 512
C_XBC = 1536
C_GATE = 3072
C_LRU = 3584
C_DT = 4096
N_PROJ = 4224

LANES = 128
SUBLANES = 8
SUBCHUNK = 128
CONV_TILE = SUBLANES
POOL_TILE = 2 * SUBLANES
VMEM_LIMIT = 56 * 1024 * 1024


def _cparams(sem):
    return pltpu.CompilerParams(dimension_semantics=sem, vmem_limit_bytes=VMEM_LIMIT)


def _resident(shape):
    nd = len(shape)
    return pl.BlockSpec(shape, lambda *_: (0,) * nd, pipeline_mode=pl.Buffered(1))


def _rms(x, g):
    var = jnp.mean(x * x, axis=-1, keepdims=True)
    return x * lax.rsqrt(var + EPS) * g


def _silu(x):
    return x * jax.nn.sigmoid(x)


def _softplus(x):
    return jnp.maximum(x, 0.0) + jnp.log1p(jnp.exp(-jnp.abs(x)))


def _gelu_tanh(x):
    c = math.sqrt(2.0 / math.pi)
    return x * (0.5 * (1.0 + jnp.tanh(c * (x + 0.044715 * (x * x * x)))))


def _split3(x):
    x1 = x.astype(BF16)
    r1 = x - x1.astype(F32)
    x2 = r1.astype(BF16)
    x3 = (r1 - x2.astype(F32)).astype(BF16)
    return x1, x2, x3


def _sel_dot(sel, x):
    x1, x2, x3 = _split3(x)
    d = functools.partial(jnp.dot, preferred_element_type=F32)
    return d(sel, x1) + d(sel, x2) + d(sel, x3)


def _dot_sel(x, sel):
    x1, x2, x3 = _split3(x)
    d = functools.partial(jnp.dot, preferred_element_type=F32)
    return d(x1, sel) + d(x2, sel) + d(x3, sel)


def _col_chunks(n, step=512):
    return [(j, min(step, n - j)) for j in range(0, n, step)]


def _linear_kernel(*refs, has_gain, has_res):
    x_ref, w_ref = refs[0], refs[1]
    k = 2
    g_ref = r_ref = None
    if has_gain:
        g_ref = refs[k]; k += 1
    if has_res:
        r_ref = refs[k]; k += 1
    o_ref = refs[k]
    x = x_ref[...]
    if has_gain:
        x = _rms(x.astype(F32), g_ref[...])
    xb = x.astype(BF16)
    for j, n in _col_chunks(o_ref.shape[1]):
        y = jnp.dot(xb, w_ref[:, j:j + n], preferred_element_type=F32)
        if has_res:
            y = y + r_ref[:, j:j + n]
        o_ref[:, j:j + n] = y.astype(o_ref.dtype)


def _linear(x, w, *, name, gain=None, residual=None, out_dtype=F32, tm=512):
    rows, kdim = x.shape
    n = w.shape[1]
    tm = min(tm, rows)
    assert rows % tm == 0
    in_specs = [pl.BlockSpec((tm, kdim), lambda i: (i, 0)), _resident(w.shape)]
    args = [x, w]
    if gain is not None:
        in_specs.append(_resident((1, kdim)))
        args.append(gain.reshape(1, kdim))
    if residual is not None:
        in_specs.append(pl.BlockSpec((tm, n), lambda i: (i, 0)))
        args.append(residual)
    return pl.pallas_call(
        functools.partial(_linear_kernel, has_gain=gain is not None, has_res=residual is not None),
        grid=(rows // tm,),
        in_specs=in_specs,
        out_specs=pl.BlockSpec((tm, n), lambda i: (i, 0)),
        out_shape=jax.ShapeDtypeStruct((rows, n), out_dtype),
        name=name,
        compiler_params=_cparams(("parallel",)),
    )(*args)


def _kv_kernel(x_ref, wk_ref, wv_ref, k_ref, v_ref):
    xb = x_ref[...].astype(BF16)
    k_ref[...] = jnp.dot(xb, wk_ref[...], preferred_element_type=F32)
    v_ref[...] = jnp.dot(xb, wv_ref[...], preferred_element_type=F32)


def _kv_proj(mem2d, wk, wv, tm=512):
    rows = mem2d.shape[0]
    tm = min(tm, rows)
    wspec = pl.BlockSpec((None, D_MODEL, D_MODEL), lambda l, i: (l, 0, 0))
    ospec = pl.BlockSpec((None, tm, D_MODEL), lambda l, i: (l, i, 0))
    oshape = jax.ShapeDtypeStruct((DEPTH, rows, D_MODEL), F32)
    return pl.pallas_call(
        _kv_kernel,
        grid=(DEPTH, rows // tm),
        in_specs=[pl.BlockSpec((tm, D_MODEL), lambda l, i: (i, 0)), wspec, wspec],
        out_specs=[ospec, ospec],
        out_shape=[oshape, oshape],
        name="kv_proj",
        compiler_params=_cparams(("parallel", "parallel")),
    )(mem2d, wk, wv)


def _ffn_kernel(x_ref, g_ref, wg_ref, wu_ref, wd_ref, o_ref):
    x = x_ref[...]
    ub = _rms(x, g_ref[...]).astype(BF16)
    acc = x
    for j, n in _col_chunks(D_FF, 1408):
        gate = jnp.dot(ub, wg_ref[:, j:j + n], preferred_element_type=F32)
        up = jnp.dot(ub, wu_ref[:, j:j + n], preferred_element_type=F32)
        act = (_silu(gate) * up).astype(BF16)
        acc = acc + jnp.dot(act, wd_ref[j:j + n, :], preferred_element_type=F32)
    o_ref[...] = acc


def _ffn(h, gain, wg, wu, wd, tm=512):
    rows = h.shape[0]
    tm = min(tm, rows)
    row_spec = pl.BlockSpec((tm, D_MODEL), lambda i: (i, 0))
    return pl.pallas_call(
        _ffn_kernel,
        grid=(rows // tm,),
        in_specs=[row_spec, _resident((1, D_MODEL)), _resident(wg.shape), _resident(wu.shape),
                  _resident(wd.shape)],
        out_specs=row_spec,
        out_shape=jax.ShapeDtypeStruct((rows, D_MODEL), F32),
        name="ffn",
        compiler_params=_cparams(("parallel",)),
    )(h, gain.reshape(1, D_MODEL), wg, wu, wd)


def _norm_kernel(x_ref, g_ref, o_ref):
    o_ref[...] = _rms(x_ref[...], g_ref[...])


def _final_norm(h, gain, tm=512):
    rows = h.shape[0]
    tm = min(tm, rows)
    row_spec = pl.BlockSpec((tm, D_MODEL), lambda i: (i, 0))
    return pl.pallas_call(
        _norm_kernel,
        grid=(rows // tm,),
        in_specs=[row_spec, _resident((1, D_MODEL))],
        out_specs=row_spec,
        out_shape=jax.ShapeDtypeStruct((rows, D_MODEL), F32),
        name="final_norm",
        compiler_params=_cparams(("parallel",)),
    )(h, gain.reshape(1, D_MODEL))


def _attn_kernel(q_ref, k_ref, v_ref, o_ref, *, nseq, tq):
    scale = MEM_HDIM ** -0.5
    for s in range(nseq):
        rows = slice(s * tq, (s + 1) * tq)
        for hd in range(MEM_HEADS):
            cols = slice(hd * MEM_HDIM, (hd + 1) * MEM_HDIM)
            qh = q_ref[rows, cols]
            kh = k_ref[s, :, cols].astype(BF16)
            vh = v_ref[s, :, cols].astype(BF16)
            sc = lax.dot_general(qh, kh, (((1,), (1,)), ((), ())), preferred_element_type=F32) * scale
            m = jnp.max(sc, axis=-1, keepdims=True)
            e = jnp.exp(sc - m)
            p = e / jnp.sum(e, axis=-1, keepdims=True)
            o = jnp.dot(p.astype(BF16), vh, preferred_element_type=F32)
            o_ref[rows, cols] = o.astype(o_ref.dtype)


def _attn(q, k, v, layer, *, nseq_blk, tq, seq_rows):
    rows = q.shape[0]
    nseq = rows // seq_rows
    qblocks = seq_rows // tq
    assert nseq_blk == 1 or qblocks == 1
    q_spec = pl.BlockSpec((nseq_blk * tq, D_MODEL), lambda i, c: (i * qblocks + c, 0))
    kv_spec = pl.BlockSpec((None, nseq_blk, N_MEM, D_MODEL), lambda i, c: (layer, i, 0, 0))
    return pl.pallas_call(
        functools.partial(_attn_kernel, nseq=nseq_blk, tq=tq),
        grid=(nseq // nseq_blk, qblocks),
        in_specs=[q_spec, kv_spec, kv_spec],
        out_specs=q_spec,
        out_shape=jax.ShapeDtypeStruct((rows, D_MODEL), BF16),
        name="attn_seq%d" % nseq_blk,
        compiler_params=_cparams(("parallel", "arbitrary")),
    )(q, k, v)


def _mix_kernel(proj_ref, pool_st, sconv_st, ssd_st, lconv_st, lru_st,
                pool_w, pool_scale, sconv_w, sconv_b, dt_bias, a_log, d_skip, ssd_norm,
                lconv_w, lconv_b, gate_w, gate_b, lam, expand,
                ycat_ref, pool_out, sconv_out, ssd_out, lconv_out, lru_out,
                ext_pool, ext_sconv, ext_lconv, lru_carry, h_state,
                *, S, T, NC, LV, pos0):
    R = S * T
    c = pl.program_id(1)
    carried = NC > 1

    @pl.when(c == 0)
    def _load_state():
        ext_pool[:, 0:POOL_TILE, :] = pool_st[...]
        ext_sconv[:, 0:CONV_TILE, :] = sconv_st[...]
        ext_lconv[:, 0:CONV_TILE, :] = lconv_st[...]
        lru_carry[...] = lru_st[...]
        if carried:
            for g in range(SSD_GROUPS):
                h_state[0, g] = ssd_st[0, g * SSD_GWIDTH:(g + 1) * SSD_GWIDTH, :].T

    x = proj_ref[...]
    row = lax.broadcasted_iota(jnp.int32, (R, 1), 0)
    t_in = row % T
    pos = pos0 + c * LV + t_in

    u_pool = x[:, C_POOL:C_POOL + POOL_WIDTH]
    ext_pool[:, POOL_TILE:, :] = u_pool.reshape(S, T, POOL_WIDTH)
    for g, w in enumerate(POOL_WINDOWS):
        cols = slice(g * POOL_GDIM, (g + 1) * POOL_GDIM)
        acc = ext_pool[:, pl.ds(POOL_TILE, T), cols]
        for j in range(1, w):
            acc = acc + ext_pool[:, pl.ds(POOL_TILE - j, T), cols]
        cnt = jnp.minimum(pos + 1, w).astype(F32)
        pooled = acc.reshape(R, POOL_GDIM) / cnt - u_pool[:, cols]
        y = jnp.dot(pooled.astype(BF16), pool_w[g], preferred_element_type=F32)
        ycat_ref[:, cols] = (y * pool_scale[:, cols]).astype(ycat_ref.dtype)
    new_pool = ext_pool[:, pl.ds(LV, POOL_TILE), :]
    ext_pool[:, 0:POOL_TILE, :] = new_pool

    ext_sconv[:, CONV_TILE:, :] = x[:, C_XBC:C_XBC + SSD_CONV_DIM].reshape(S, T, SSD_CONV_DIM)
    conv = sconv_b[...].reshape(1, 1, SSD_CONV_DIM)
    for k in range(CONV_WIDTH):
        conv = conv + sconv_w[k:k + 1, :].reshape(1, 1, SSD_CONV_DIM) * ext_sconv[:, pl.ds(CONV_TILE - 3 + k, T), :]
    new_sconv = ext_sconv[:, pl.ds(LV, CONV_TILE), :]
    ext_sconv[:, 0:CONV_TILE, :] = new_sconv
    xbc = _silu(conv).reshape(R, SSD_CONV_DIM)

    lane = lax.broadcasted_iota(jnp.int32, (1, LANES), 1)
    dt = _softplus(x[:, C_DT:C_DT + LANES] + dt_bias[...])
    dt = jnp.where((lane < SSD_HEADS) & (t_in < LV), dt, 0.0)
    d_a = dt * (-jnp.exp(a_log[...]))
    z = x[:, C_Z:C_Z + SSD_WIDTH]

    li = lax.broadcasted_iota(jnp.int32, (SUBCHUNK, SUBCHUNK), 0)
    si = lax.broadcasted_iota(jnp.int32, (SUBCHUNK, SUBCHUNK), 1)
    same_seq = (li // T) == (si // T) if S > 1 else (li >= 0)
    causal = same_seq & (si <= li)
    causal_b = causal.astype(BF16)
    same_b = same_seq.astype(BF16)
    lane_q = lax.broadcasted_iota(jnp.int32, (SUBCHUNK, LANES), 1)
    row_q = lax.broadcasted_iota(jnp.int32, (SUBCHUNK, 1), 0)
    expand_m = expand[...]

    for q in range(R // SUBCHUNK):
        rs = slice(q * SUBCHUNK, (q + 1) * SUBCHUNK)
        xs = xbc[rs, 0:SSD_WIDTH]
        d_a_q = d_a[rs]
        acs = _sel_dot(causal_b, d_a_q)
        tot = _sel_dot(same_b, d_a_q)
        acs_row = acs.T
        dt_e = _dot_sel(dt[rs], expand_m)
        acs_e = _dot_sel(acs, expand_m)
        tot_e = _dot_sel(tot, expand_m)
        eacs_e = jnp.exp(acs_e)
        etot_e = jnp.exp(tot_e)
        xdt = xs * dt_e
        xdt_b = xdt.astype(BF16)
        xdecay_b = (xdt * jnp.exp(tot_e - acs_e)).astype(BF16)
        y_parts = []
        for g in range(SSD_GROUPS):
            gc = slice(g * SSD_GWIDTH, (g + 1) * SSD_GWIDTH)
            b_g = xbc[rs, SSD_WIDTH + g * SSD_STATE:SSD_WIDTH + (g + 1) * SSD_STATE].astype(BF16)
            c_g = xbc[rs, SSD_WIDTH + (SSD_GROUPS + g) * SSD_STATE:
                      SSD_WIDTH + (SSD_GROUPS + g + 1) * SSD_STATE].astype(BF16)
            cb = lax.dot_general(c_g, b_g, (((1,), (1,)), ((), ())), preferred_element_type=F32)
            diag = []
            for j in range(SSD_GWIDTH // LANES):
                pair = []
                for hh in range(2):
                    k = g * (SSD_HEADS // SSD_GROUPS) + 2 * j + hh
                    seg = acs[:, k:k + 1] - acs_row[k:k + 1, :]
                    lmat = jnp.where(causal, jnp.exp(jnp.where(causal, seg, 0.0)), 0.0)
                    pair.append((cb * lmat).astype(BF16))
                xp = xdt_b[:, g * SSD_GWIDTH + j * LANES:g * SSD_GWIDTH + (j + 1) * LANES]
                top = jnp.where(lane_q < SSD_HEAD_DIM, xp, jnp.zeros_like(xp))
                bot = jnp.where(lane_q >= SSD_HEAD_DIM, xp, jnp.zeros_like(xp))
                diag.append(jnp.dot(jnp.concatenate(pair, axis=1), jnp.concatenate([top, bot], axis=0),
                                    preferred_element_type=F32))
            y_g = jnp.concatenate(diag, axis=1)
            if carried:
                h_t = h_state[0, g]
                y_g = y_g + jnp.dot(c_g, h_t.astype(BF16), preferred_element_type=F32) * eacs_e[:, gc]
                upd = lax.dot_general(b_g, xdecay_b[:, gc], (((0,), (0,)), ((), ())),
                                      preferred_element_type=F32)
                h_state[0, g] = h_t * etot_e[0:1, gc] + upd
            else:
                y_off = jnp.zeros((SUBCHUNK, SSD_GWIDTH), F32)
                for s in range(S):
                    mine = (row_q // T) == s
                    h_t = ssd_st[s, gc, :].T
                    c_s = jnp.where(mine, c_g, jnp.zeros_like(c_g))
                    b_s = jnp.where(mine, b_g, jnp.zeros_like(b_g))
                    y_off = y_off + jnp.dot(c_s, h_t.astype(BF16), preferred_element_type=F32)
                    upd = lax.dot_general(b_s, xdecay_b[:, gc], (((0,), (0,)), ((), ())),
                                          preferred_element_type=F32)
                    ssd_out[s, gc, :] = (h_t * etot_e[s * T:s * T + 1, gc] + upd).T
                y_g = y_g + y_off * eacs_e[:, gc]
            y_parts.append(y_g)
        y = jnp.concatenate(y_parts, axis=1) + xs * d_skip[...]
        y = _rms(y * _silu(z[rs]), ssd_norm[...])
        ycat_ref[rs, POOL_WIDTH:POOL_WIDTH + SSD_WIDTH] = y.astype(ycat_ref.dtype)

    ext_lconv[:, CONV_TILE:, :] = x[:, C_LRU:C_LRU + LRU_WIDTH].reshape(S, T, LRU_WIDTH)
    xc = lconv_b[...].reshape(1, 1, LRU_WIDTH)
    for k in range(CONV_WIDTH):
        xc = xc + lconv_w[k:k + 1, :].reshape(1, 1, LRU_WIDTH) * ext_lconv[:, pl.ds(CONV_TILE - 3 + k, T), :]
    new_lconv = ext_lconv[:, pl.ds(LV, CONV_TILE), :]
    ext_lconv[:, 0:CONV_TILE, :] = new_lconv
    xc = xc.reshape(R, LRU_WIDTH)
    gates = jnp.dot(xc.astype(BF16), gate_w[...], preferred_element_type=F32) + gate_b[...]
    r_gate = jax.nn.sigmoid(gates[:, 0:LRU_WIDTH])
    i_gate = jax.nn.sigmoid(gates[:, LRU_WIDTH:2 * LRU_WIDTH])
    log_a = (-LRU_C) * r_gate * _softplus(-lam[...])
    a = jnp.exp(log_a)
    mult = jnp.where(pos == 0, 1.0, jnp.sqrt(1.0 - jnp.exp(2.0 * log_a)))
    b = mult * i_gate * xc
    d = 1
    while d < T:
        keep = t_in >= d
        a_prev = jnp.where(keep, pltpu.roll(a, d, 0), 1.0)
        b_prev = jnp.where(keep, pltpu.roll(b, d, 0), 0.0)
        b = a * b_prev + b
        a = a * a_prev
        d *= 2
    if S == 1:
        h_in = lru_carry[0, 0:1, :]
    else:
        h_in = lru_carry[...].reshape(R, LRU_WIDTH)
    h = a * h_in + b
    gate = x[:, C_GATE:C_GATE + LRU_WIDTH]
    ycat_ref[:, POOL_WIDTH + SSD_WIDTH:D_MIX] = (h * _gelu_tanh(gate)).astype(ycat_ref.dtype)
    if S == 1:
        lru_carry[0] = jnp.broadcast_to(h[LV - 1:LV, :], (SUBLANES, LRU_WIDTH))
    else:
        li_r = lax.broadcasted_iota(jnp.int32, (R, R), 0)
        si_r = lax.broadcasted_iota(jnp.int32, (R, R), 1)
        pick_last = ((li_r // T) == (si_r // T)) & ((si_r % T) == LV - 1)
        lru_carry[...] = _sel_dot(pick_last.astype(BF16), h).reshape(S, T, LRU_WIDTH)

    @pl.when(c == NC - 1)
    def _store_state():
        pool_out[...] = ext_pool[:, 0:POOL_TILE, :]
        sconv_out[...] = ext_sconv[:, 0:CONV_TILE, :]
        lconv_out[...] = ext_lconv[:, 0:CONV_TILE, :]
        lru_out[...] = lru_carry[...]
        if carried:
            for g in range(SSD_GROUPS):
                ssd_out[0, g * SSD_GWIDTH:(g + 1) * SSD_GWIDTH, :] = h_state[0, g].T


def _mix(proj, st, ssd_state, ssd_layer, w, *, S, T, NC, LV, pos0):
    rows = proj.shape[0]
    nseq = rows // (NC * T)
    R = S * T
    assert nseq % S == 0 and R % SUBCHUNK == 0
    assert (S == 1) or (NC == 1 and R == SUBCHUNK and T == SUBLANES)
    seq3 = lambda i, c: (i, 0, 0)
    in_specs = [
        pl.BlockSpec((R, N_PROJ), lambda i, c: (i * NC + c, 0)),
        pl.BlockSpec((S, POOL_TILE, POOL_WIDTH), seq3),
        pl.BlockSpec((S, CONV_TILE, SSD_CONV_DIM), seq3),
        pl.BlockSpec((None, S, SSD_WIDTH, SSD_STATE), lambda i, c: (ssd_layer, i, 0, 0)),
        pl.BlockSpec((S, CONV_TILE, LRU_WIDTH), seq3),
        pl.BlockSpec((S, SUBLANES, LRU_WIDTH), seq3),
    ] + [_resident(a.shape) for a in w]
    out_specs = [
        pl.BlockSpec((R, D_MIX), lambda i, c: (i * NC + c, 0)),
        pl.BlockSpec((S, POOL_TILE, POOL_WIDTH), seq3),
        pl.BlockSpec((S, CONV_TILE, SSD_CONV_DIM), seq3),
        pl.BlockSpec((S, SSD_WIDTH, SSD_STATE), seq3),
        pl.BlockSpec((S, CONV_TILE, LRU_WIDTH), seq3),
        pl.BlockSpec((S, SUBLANES, LRU_WIDTH), seq3),
    ]
    out_shape = [
        jax.ShapeDtypeStruct((rows, D_MIX), BF16),
        jax.ShapeDtypeStruct((nseq, POOL_TILE, POOL_WIDTH), F32),
        jax.ShapeDtypeStruct((nseq, CONV_TILE, SSD_CONV_DIM), F32),
        jax.ShapeDtypeStruct((nseq, SSD_WIDTH, SSD_STATE), F32),
        jax.ShapeDtypeStruct((nseq, CONV_TILE, LRU_WIDTH), F32),
        jax.ShapeDtypeStruct((nseq, SUBLANES, LRU_WIDTH), F32),
    ]
    h_scratch = (1, SSD_GROUPS, SSD_STATE, SSD_GWIDTH) if NC > 1 else (1, 1, SUBLANES, LANES)
    scratch = [
        pltpu.VMEM((S, POOL_TILE + T, POOL_WIDTH), F32),
        pltpu.VMEM((S, CONV_TILE + T, SSD_CONV_DIM), F32),
        pltpu.VMEM((S, CONV_TILE + T, LRU_WIDTH), F32),
        pltpu.VMEM((S, SUBLANES, LRU_WIDTH), F32),
        pltpu.VMEM(h_scratch, F32),
    ]
    return pl.pallas_call(
        functools.partial(_mix_kernel, S=S, T=T, NC=NC, LV=LV, pos0=pos0),
        grid=(nseq // S, NC),
        in_specs=in_specs,
        out_specs=out_specs,
        out_shape=out_shape,
        scratch_shapes=scratch,
        name="mix_seq%d" % S,
        compiler_params=_cparams(("parallel", "arbitrary")),
    )(proj, st[0], st[1], ssd_state, st[2], st[3], *w)


def _expand_matrix():
    e = np.zeros((LANES, SSD_WIDTH), np.float32)
    for k in range(SSD_HEADS):
        e[k, k * SSD_HEAD_DIM:(k + 1) * SSD_HEAD_DIM] = 1.0
    return jnp.asarray(e, BF16)


def _block_diag(w):
    eye = jnp.eye(LRU_HEADS, dtype=w.dtype)
    return jnp.einsum('hij,hg->higj', w, eye).reshape(LRU_WIDTH, LRU_WIDTH)


def _pad_lanes(v):
    return jnp.pad(v, (0, LANES - v.shape[0])).reshape(1, LANES)


def _layer_params(l, P):
    w_in = P['w_in'][l]
    off_dt = 3072
    w_in = jnp.concatenate(
        [w_in[:, :off_dt], w_in[:, off_dt + SSD_HEADS:], w_in[:, off_dt:off_dt + SSD_HEADS],
         jnp.zeros((D_MODEL, N_PROJ - C_DT - SSD_HEADS), F32)], axis=1).astype(BF16)
    mix_w = (
        P['pool_w'][l].astype(BF16),
        P['pool_scale'][l].reshape(1, POOL_WIDTH),
        P['ssd_conv_w'][l],
        P['ssd_conv_b'][l].reshape(1, SSD_CONV_DIM),
        _pad_lanes(P['ssd_dt_bias'][l]),
        _pad_lanes(P['ssd_a_log'][l]),
        jnp.repeat(P['ssd_d'][l], SSD_HEAD_DIM).reshape(1, SSD_WIDTH),
        P['ssd_norm'][l].reshape(1, SSD_WIDTH),
        P['lru_conv_w'][l],
        P['lru_conv_b'][l].reshape(1, LRU_WIDTH),
        jnp.concatenate([_block_diag(P['lru_wa'][l]), _block_diag(P['lru_wx'][l])], axis=1).astype(BF16),
        jnp.concatenate([P['lru_ba'][l].reshape(1, LRU_WIDTH), P['lru_bx'][l].reshape(1, LRU_WIDTH)], axis=1),
        P['lru_lambda'][l].reshape(1, LRU_WIDTH),
        _expand_matrix(),
    )
    return dict(
        w_in=w_in, mix=mix_w, norm_mix=P['norm_mix'][l],
        w_out=P['w_out'][l].astype(BF16),
        norm_mem=P['norm_mem'][l], wq=P['w_mem_q'][l].astype(BF16), wo=P['w_mem_o'][l].astype(BF16),
        norm_ffn=P['norm_ffn'][l], wg=P['w_ffn_gate'][l].astype(BF16), wu=P['w_ffn_up'][l].astype(BF16),
        wd=P['w_ffn_down'][l].astype(BF16),
    )


def _run_trunk(h, layers, norm_final, states, ssd_state, mem_k, mem_v, *, mix_cfg, attn_cfg):
    new_states = []
    for l, lp in enumerate(layers):
        proj = _linear(h, lp['w_in'], name="in_proj", gain=lp['norm_mix'])
        ycat, *st = _mix(proj, states[l], ssd_state, min(l, ssd_state.shape[0] - 1), lp['mix'], **mix_cfg)
        h = _linear(ycat, lp['w_out'], name="out_proj", residual=h)
        q = _linear(h, lp['wq'], name="q_proj", gain=lp['norm_mem'], out_dtype=BF16)
        o = _attn(q, mem_k, mem_v, l, **attn_cfg)
        h = _linear(o, lp['wo'], name="o_proj", residual=h)
        h = _ffn(h, lp['norm_ffn'], lp['wg'], lp['wu'], lp['wd'])
        new_states.append(st)
    return _final_norm(h, norm_final), new_states


def _unpack_states(new_states, nseq):
    pool = jnp.stack([s[0][:, 1:POOL_TILE] for s in new_states])
    sconv = jnp.stack([s[1][:, CONV_TILE - 3:CONV_TILE] for s in new_states])
    ssd = jnp.stack([s[2] for s in new_states]).reshape(DEPTH, nseq, SSD_HEADS, SSD_HEAD_DIM, SSD_STATE)
    lconv = jnp.stack([s[3][:, CONV_TILE - 3:CONV_TILE] for s in new_states])
    lru = jnp.stack([s[4][:, 0] for s in new_states])
    return pool, sconv, ssd, lconv, lru


def _trunk_prompt(x_prompt, mem_prompt, P, layers, *, chunk=256, tq=512):
    batch, seq, _ = x_prompt.shape
    chunk = min(chunk, seq)
    tq = min(tq, seq)
    wk = P['w_mem_k'].astype(BF16)
    wv = P['w_mem_v'].astype(BF16)
    mem_k, mem_v = _kv_proj(mem_prompt.reshape(batch * N_MEM, D_MODEL), wk, wv)
    mem_k = mem_k.reshape(DEPTH, batch, N_MEM, D_MODEL)
    mem_v = mem_v.reshape(DEPTH, batch, N_MEM, D_MODEL)
    zeros = (jnp.zeros((batch, POOL_TILE, POOL_WIDTH), F32), jnp.zeros((batch, CONV_TILE, SSD_CONV_DIM), F32),
             jnp.zeros((batch, CONV_TILE, LRU_WIDTH), F32), jnp.zeros((batch, SUBLANES, LRU_WIDTH), F32))
    ssd0 = jnp.zeros((1, batch, SSD_WIDTH, SSD_STATE), F32)
    y, new_states = _run_trunk(
        x_prompt.reshape(batch * seq, D_MODEL), layers, P['norm_final'], [zeros] * DEPTH, ssd0, mem_k, mem_v,
        mix_cfg=dict(S=1, T=chunk, NC=seq // chunk, LV=chunk, pos0=0),
        attn_cfg=dict(nseq_blk=1, tq=tq, seq_rows=seq))
    return (y.reshape(batch, seq, D_MODEL),) + _unpack_states(new_states, batch) + (
        mem_k.reshape(DEPTH, batch, N_MEM, MEM_HEADS, MEM_HDIM),
        mem_v.reshape(DEPTH, batch, N_MEM, MEM_HEADS, MEM_HDIM))


def _trunk_sample(x_sample, past_len, state_pool, state_ssd_conv, state_ssd, state_lru_conv, state_lru,
                  cache_k, cache_v, P, layers, *, seq_blk=16, attn_blk=4):
    nseq, seq, _ = x_sample.shape
    assert seq <= SUBLANES
    pad_t = SUBLANES - seq
    h = jnp.pad(x_sample, ((0, 0), (0, pad_t), (0, 0))).reshape(nseq * SUBLANES, D_MODEL)
    states = []
    for l in range(DEPTH):
        states.append((
            jnp.pad(state_pool[l], ((0, 0), (POOL_TILE - POOL_BUF, 0), (0, 0))),
            jnp.pad(state_ssd_conv[l], ((0, 0), (CONV_TILE - 3, 0), (0, 0))),
            jnp.pad(state_lru_conv[l], ((0, 0), (CONV_TILE - 3, 0), (0, 0))),
            jnp.broadcast_to(state_lru[l][:, None, :], (nseq, SUBLANES, LRU_WIDTH)),
        ))
    y, new_states = _run_trunk(
        h, layers, P['norm_final'], states, state_ssd.reshape(DEPTH, nseq, SSD_WIDTH, SSD_STATE),
        cache_k.reshape(DEPTH, nseq, N_MEM, D_MODEL), cache_v.reshape(DEPTH, nseq, N_MEM, D_MODEL),
        mix_cfg=dict(S=seq_blk, T=SUBLANES, NC=1, LV=seq, pos0=past_len),
        attn_cfg=dict(nseq_blk=attn_blk, tq=SUBLANES, seq_rows=SUBLANES))
    y = y.reshape(nseq, SUBLANES, D_MODEL)[:, :seq]
    return (y,) + _unpack_states(new_states, nseq)


def kernel(x_prompt, x_sample, mem_prompt, state_pool, state_ssd_conv, state_ssd, state_lru_conv, state_lru,
           cache_mem_k, cache_mem_v, norm_mix, w_in, pool_w, pool_scale, ssd_conv_w, ssd_conv_b, ssd_dt_bias,
           ssd_a_log, ssd_d, ssd_norm, lru_conv_w, lru_conv_b, lru_wa, lru_ba, lru_wx, lru_bx, lru_lambda,
           w_out, norm_mem, w_mem_q, w_mem_k, w_mem_v, w_mem_o, norm_ffn, w_ffn_gate, w_ffn_up, w_ffn_down,
           norm_final):
    P = dict(norm_mix=norm_mix, w_in=w_in, pool_w=pool_w, pool_scale=pool_scale, ssd_conv_w=ssd_conv_w,
             ssd_conv_b=ssd_conv_b, ssd_dt_bias=ssd_dt_bias, ssd_a_log=ssd_a_log, ssd_d=ssd_d, ssd_norm=ssd_norm,
             lru_conv_w=lru_conv_w, lru_conv_b=lru_conv_b, lru_wa=lru_wa, lru_ba=lru_ba, lru_wx=lru_wx,
             lru_bx=lru_bx, lru_lambda=lru_lambda, w_out=w_out, norm_mem=norm_mem, w_mem_q=w_mem_q,
             w_mem_k=w_mem_k, w_mem_v=w_mem_v, w_mem_o=w_mem_o, norm_ffn=norm_ffn, w_ffn_gate=w_ffn_gate,
             w_ffn_up=w_ffn_up, w_ffn_down=w_ffn_down, norm_final=norm_final)
    layers = [_layer_params(l, P) for l in range(DEPTH)]
    (y_prompt, p_pool, p_sconv, p_ssd, p_lconv, p_lru, p_mem_k, p_mem_v) = _trunk_prompt(
        x_prompt, mem_prompt, P, layers)
    (y_sample, s_pool, s_sconv, s_ssd, s_lconv, s_lru) = _trunk_sample(
        x_sample, PAST_LEN, state_pool, state_ssd_conv, state_ssd, state_lru_conv, state_lru,
        cache_mem_k, cache_mem_v, P, layers)
    return (y_prompt, y_sample, p_pool, p_sconv, p_ssd, p_lconv, p_lru, p_mem_k, p_mem_v,
            s_pool, s_sconv, s_ssd, s_lconv, s_lru)
```

```python
import functools
import math

import numpy as np
import jax
import jax.numpy as jnp
from jax import lax
from jax.experimental import pallas as pl
from jax.experimental.pallas import tpu as pltpu

F32 = jnp.float32
BF16 = jnp.bfloat16

D_MODEL = 1024
DEPTH = 4
EPS = 1e-6
POOL_WIDTH = 512
POOL_WINDOWS = (2, 4, 8, 16)
POOL_GDIM = 128
POOL_BUF = 15
SSD_WIDTH = 1024
SSD_HEAD_DIM = 64
SSD_HEADS = 16
SSD_GROUPS = 2
SSD_GWIDTH = SSD_WIDTH // SSD_GROUPS
SSD_STATE = 128
SSD_CONV_DIM = 1536
CONV_WIDTH = 4
LRU_WIDTH = 512
LRU_HEADS = 8
LRU_HDIM = 64
LRU_C = 8.0
N_MEM = 256
MEM_HEADS = 4
MEM_HDIM = 256
D_FF = 2816
D_MIX = 2048
PAST_LEN = 16384

C_POOL = 0
C_Z = 512
C_XBC = 1536
C_GATE = 3072
C_LRU = 3584
C_DT = 4096
N_PROJ = 4224

LANES = 128
SUBLANES = 8
SUBCHUNK = 128
CONV_TILE = SUBLANES
POOL_TILE = 2 * SUBLANES
VMEM_LIMIT = 56 * 1024 * 1024


def _cparams(sem):
    return pltpu.CompilerParams(dimension_semantics=sem, vmem_limit_bytes=VMEM_LIMIT)


def _resident(shape):
    nd = len(shape)
    return pl.BlockSpec(shape, lambda *_: (0,) * nd, pipeline_mode=pl.Buffered(1))


def _rms(x, g):
    var = jnp.mean(x * x, axis=-1, keepdims=True)
    return x * lax.rsqrt(var + EPS) * g


def _silu(x):
    return x * jax.nn.sigmoid(x)


def _softplus(x):
    return jnp.maximum(x, 0.0) + jnp.log1p(jnp.exp(-jnp.abs(x)))


def _gelu_tanh(x):
    c = math.sqrt(2.0 / math.pi)
    return x * (0.5 * (1.0 + jnp.tanh(c * (x + 0.044715 * (x * x * x)))))


def _split3(x):
    x1 = x.astype(BF16)
    r1 = x - x1.astype(F32)
    x2 = r1.astype(BF16)
    x3 = (r1 - x2.astype(F32)).astype(BF16)
    return x1, x2, x3


def _sel_dot(sel, x):
    x1, x2, x3 = _split3(x)
    d = functools.partial(jnp.dot, preferred_element_type=F32)
    return d(sel, x1) + d(sel, x2) + d(sel, x3)


def _spread(x, sel2):
    hi = x.astype(BF16)
    lo = (x - hi.astype(F32)).astype(BF16)
    return jnp.dot(jnp.concatenate([hi, lo], axis=1), sel2, preferred_element_type=F32)


def _col_chunks(n, step=512):
    return [(j, min(step, n - j)) for j in range(0, n, step)]


def _linear_kernel(*refs, has_gain, has_res):
    x_ref, w_ref = refs[0], refs[1]
    k = 2
    g_ref = r_ref = None
    if has_gain:
        g_ref = refs[k]; k += 1
    if has_res:
        r_ref = refs[k]; k += 1
    o_ref = refs[k]
    x = x_ref[...]
    if has_gain:
        x = _rms(x.astype(F32), g_ref[...])
    xb = x.astype(BF16)
    for j, n in _col_chunks(o_ref.shape[1]):
        y = jnp.dot(xb, w_ref[:, j:j + n], preferred_element_type=F32)
        if has_res:
            y = y + r_ref[:, j:j + n]
        o_ref[:, j:j + n] = y.astype(o_ref.dtype)


def _linear(x, w, *, name, gain=None, residual=None, out_dtype=F32, tm=512):
    rows, kdim = x.shape
    n = w.shape[1]
    tm = min(tm, rows)
    assert rows % tm == 0
    in_specs = [pl.BlockSpec((tm, kdim), lambda i: (i, 0)), _resident(w.shape)]
    args = [x, w]
    if gain is not None:
        in_specs.append(_resident((1, kdim)))
        args.append(gain.reshape(1, kdim))
    if residual is not None:
        in_specs.append(pl.BlockSpec((tm, n), lambda i: (i, 0)))
        args.append(residual)
    return pl.pallas_call(
        functools.partial(_linear_kernel, has_gain=gain is not None, has_res=residual is not None),
        grid=(rows // tm,),
        in_specs=in_specs,
        out_specs=pl.BlockSpec((tm, n), lambda i: (i, 0)),
        out_shape=jax.ShapeDtypeStruct((rows, n), out_dtype),
        name=name,
        compiler_params=_cparams(("parallel",)),
    )(*args)


def _kv_kernel(x_ref, wk_ref, wv_ref, k_ref, v_ref):
    xb = x_ref[...].astype(BF16)
    k_ref[...] = jnp.dot(xb, wk_ref[...], preferred_element_type=F32)
    v_ref[...] = jnp.dot(xb, wv_ref[...], preferred_element_type=F32)


def _kv_proj(mem2d, wk, wv, tm=512):
    rows = mem2d.shape[0]
    tm = min(tm, rows)
    wspec = pl.BlockSpec((None, D_MODEL, D_MODEL), lambda l, i: (l, 0, 0))
    ospec = pl.BlockSpec((None, tm, D_MODEL), lambda l, i: (l, i, 0))
    oshape = jax.ShapeDtypeStruct((DEPTH, rows, D_MODEL), F32)
    return pl.pallas_call(
        _kv_kernel,
        grid=(DEPTH, rows // tm),
        in_specs=[pl.BlockSpec((tm, D_MODEL), lambda l, i: (i, 0)), wspec, wspec],
        out_specs=[ospec, ospec],
        out_shape=[oshape, oshape],
        name="kv_proj",
        compiler_params=_cparams(("parallel", "parallel")),
    )(mem2d, wk, wv)


def _ffn_kernel(x_ref, g_ref, wg_ref, wu_ref, wd_ref, o_ref):
    x = x_ref[...]
    ub = _rms(x, g_ref[...]).astype(BF16)
    acc = x
    for j, n in _col_chunks(D_FF, 1408):
        gate = jnp.dot(ub, wg_ref[:, j:j + n], preferred_element_type=F32)
        up = jnp.dot(ub, wu_ref[:, j:j + n], preferred_element_type=F32)
        act = (_silu(gate) * up).astype(BF16)
        acc = acc + jnp.dot(act, wd_ref[j:j + n, :], preferred_element_type=F32)
    o_ref[...] = acc


def _ffn(h, gain, wg, wu, wd, tm=512):
    rows = h.shape[0]
    tm = min(tm, rows)
    row_spec = pl.BlockSpec((tm, D_MODEL), lambda i: (i, 0))
    return pl.pallas_call(
        _ffn_kernel,
        grid=(rows // tm,),
        in_specs=[row_spec, _resident((1, D_MODEL)), _resident(wg.shape), _resident(wu.shape),
                  _resident(wd.shape)],
        out_specs=row_spec,
        out_shape=jax.ShapeDtypeStruct((rows, D_MODEL), F32),
        name="ffn",
        compiler_params=_cparams(("parallel",)),
    )(h, gain.reshape(1, D_MODEL), wg, wu, wd)


def _norm_kernel(x_ref, g_ref, o_ref):
    o_ref[...] = _rms(x_ref[...], g_ref[...])


def _final_norm(h, gain, tm=512):
    rows = h.shape[0]
    tm = min(tm, rows)
    row_spec = pl.BlockSpec((tm, D_MODEL), lambda i: (i, 0))
    return pl.pallas_call(
        _norm_kernel,
        grid=(rows // tm,),
        in_specs=[row_spec, _resident((1, D_MODEL))],
        out_specs=row_spec,
        out_shape=jax.ShapeDtypeStruct((rows, D_MODEL), F32),
        name="final_norm",
        compiler_params=_cparams(("parallel",)),
    )(h, gain.reshape(1, D_MODEL))


def _attn_block_kernel(h_ref, g_ref, wq_ref, wo_ref, k_ref, v_ref, o_ref, kb_ref, vb_ref):
    scale = MEM_HDIM ** -0.5

    @pl.when(pl.program_id(1) == 0)
    def _cast_memory():
        kb_ref[...] = k_ref[0].astype(BF16)
        vb_ref[...] = v_ref[0].astype(BF16)

    x = h_ref[...]
    q = jnp.dot(_rms(x, g_ref[...]).astype(BF16), wq_ref[...], preferred_element_type=F32).astype(BF16)
    heads = []
    for hd in range(MEM_HEADS):
        cols = slice(hd * MEM_HDIM, (hd + 1) * MEM_HDIM)
        sc = lax.dot_general(q[:, cols], kb_ref[:, cols], (((1,), (1,)), ((), ())),
                             preferred_element_type=F32) * scale
        e = jnp.exp(sc - jnp.max(sc, axis=-1, keepdims=True))
        p = e / jnp.sum(e, axis=-1, keepdims=True)
        heads.append(jnp.dot(p.astype(BF16), vb_ref[:, cols], preferred_element_type=F32).astype(BF16))
    o = jnp.concatenate(heads, axis=1)
    o_ref[...] = x + jnp.dot(o, wo_ref[...], preferred_element_type=F32)


def _attn_block(h, gain, wq, wo, k, v, layer, *, tq, seq_rows):
    rows = h.shape[0]
    nseq = rows // seq_rows
    qblocks = seq_rows // tq
    h_spec = pl.BlockSpec((tq, D_MODEL), lambda i, c: (i * qblocks + c, 0))
    kv_spec = pl.BlockSpec((None, 1, N_MEM, D_MODEL), lambda i, c: (layer, i, 0, 0))
    return pl.pallas_call(
        _attn_block_kernel,
        grid=(nseq, qblocks),
        in_specs=[h_spec, _resident((1, D_MODEL)), _resident(wq.shape), _resident(wo.shape), kv_spec, kv_spec],
        out_specs=h_spec,
        out_shape=jax.ShapeDtypeStruct((rows, D_MODEL), F32),
        scratch_shapes=[pltpu.VMEM((N_MEM, D_MODEL), BF16), pltpu.VMEM((N_MEM, D_MODEL), BF16)],
        name="attn_block",
        compiler_params=_cparams(("parallel", "arbitrary")),
    )(h, gain.reshape(1, D_MODEL), wq, wo, k, v)


KV_SUB = 2 * MEM_HEADS
KV_ROWS = N_MEM * KV_SUB


def _cache_view(c):
    d, n = c.shape[:2]
    c = c.reshape(d, n, N_MEM, MEM_HEADS, MEM_HDIM // LANES, LANES)
    return jnp.transpose(c, (0, 1, 2, 4, 3, 5)).reshape(d, n, KV_ROWS, LANES)


def _cache_attn_kernel(q_ref, k_ref, v_ref, o_ref, *, nseq):
    scale = MEM_HDIM ** -0.5
    nvreg = KV_ROWS // LANES
    col_j = lax.broadcasted_iota(jnp.int32, (1, KV_ROWS), 1) % KV_SUB
    real = col_j < MEM_HEADS
    ts = []
    for s in range(nseq):
        q = q_ref[s * SUBLANES:(s + 1) * SUBLANES, :]
        pieces = []
        for j in range(KV_SUB):
            blk = (j % MEM_HEADS) * 2 + j // MEM_HEADS
            pieces.append(q[:, blk * LANES:(blk + 1) * LANES])
        qm = jnp.concatenate(pieces, axis=0).astype(BF16)
        kb = k_ref[s].astype(BF16)
        sc = lax.dot_general(qm, kb, (((1,), (1,)), ((), ())), preferred_element_type=F32)
        t = sc[0:SUBLANES]
        for j in range(1, KV_SUB):
            t = jnp.where(col_j == j, sc[j * SUBLANES:(j + 1) * SUBLANES], t)
        ts.append(t)
    t = jnp.concatenate(ts, axis=0)
    t = (t + pltpu.roll(t, KV_ROWS - MEM_HEADS, 1)) * scale
    t = jnp.where(real, t, jnp.finfo(F32).min)
    m = t[:, 0:LANES]
    for i in range(1, nvreg):
        m = jnp.maximum(m, t[:, i * LANES:(i + 1) * LANES])
    d = KV_SUB
    while d < LANES:
        m = jnp.maximum(m, pltpu.roll(m, d, 1))
        d *= 2
    e = jnp.where(real, jnp.exp(t - jnp.concatenate([m] * nvreg, axis=1)), 0.0)
    z = e[:, 0:LANES]
    for i in range(1, nvreg):
        z = z + e[:, i * LANES:(i + 1) * LANES]
    d = KV_SUB
    while d < LANES:
        z = z + pltpu.roll(z, d, 1)
        d *= 2
    z = jnp.where(real[:, 0:LANES], z, 1.0)
    p = e / jnp.concatenate([z] * nvreg, axis=1)
    p = p + pltpu.roll(p, MEM_HEADS, 1)
    for s in range(nseq):
        p_s = p[s * SUBLANES:(s + 1) * SUBLANES]
        pm = jnp.concatenate([jnp.where(col_j == j, p_s, 0.0) for j in range(KV_SUB)], axis=0).astype(BF16)
        o = jnp.dot(pm, v_ref[s].astype(BF16), preferred_element_type=F32)
        for blk in range(KV_SUB):
            j = (blk % 2) * MEM_HEADS + blk // 2
            o_ref[s * SUBLANES:(s + 1) * SUBLANES, blk * LANES:(blk + 1) * LANES] = (
                o[j * SUBLANES:(j + 1) * SUBLANES, :].astype(o_ref.dtype))


def _cache_attn(q, k, v, layer, *, nseq_blk=4):
    rows = q.shape[0]
    nseq = rows // SUBLANES
    assert q.shape[1] == KV_SUB * LANES and nseq % nseq_blk == 0
    q_spec = pl.BlockSpec((nseq_blk * SUBLANES, D_MODEL), lambda i: (i, 0))
    kv_spec = pl.BlockSpec((None, nseq_blk, KV_ROWS, LANES), lambda i: (layer, i, 0, 0))
    return pl.pallas_call(
        functools.partial(_cache_attn_kernel, nseq=nseq_blk),
        grid=(nseq // nseq_blk,),
        in_specs=[q_spec, kv_spec, kv_spec],
        out_specs=q_spec,
        out_shape=jax.ShapeDtypeStruct((rows, D_MODEL), BF16),
        name="cache_attn",
        compiler_params=_cparams(("parallel",)),
    )(q, k, v)


def _mix_chunk(x, ycat_ref, c, refs, *, S, T, NC, LV, pos0, side_work=()):
    side_work = iter(side_work)

    def tick(n=1):
        for _ in range(n):
            task = next(side_work, None)
            if task is not None:
                task()

    (pool_st, sconv_st, ssd_st, lconv_st, lru_st,
     pool_w, pool_scale, sconv_w, sconv_b, dt_bias, a_log, d_skip, ssd_norm,
     lconv_w, lconv_b, gate_w, gate_b, lam, expand, ssd_prev,
     out_ref, pool_out, sconv_out, ssd_out, lconv_out, lru_out,
     ext_pool, ext_sconv, ext_lconv, lru_carry, h_state) = refs
    R = S * T
    carried = NC > 1
    row = lax.broadcasted_iota(jnp.int32, (R, 1), 0)
    t_in = row % T
    pos = pos0 + c * LV + t_in

    u_pool = x[:, C_POOL:C_POOL + POOL_WIDTH]
    ext_pool[:, POOL_TILE:, :] = u_pool.reshape(S, T, POOL_WIDTH)
    for g, w in enumerate(POOL_WINDOWS):
        cols = slice(g * POOL_GDIM, (g + 1) * POOL_GDIM)
        win = ext_pool[:, :, cols].reshape(S * (POOL_TILE + T), POOL_GDIM)
        shift = 1
        while shift < w:
            win = win + pltpu.roll(win, shift, 0)
            shift *= 2
        acc = win.reshape(S, POOL_TILE + T, POOL_GDIM)[:, POOL_TILE:, :]
        cnt = jnp.minimum(pos + 1, w).astype(F32)
        pooled = acc.reshape(R, POOL_GDIM) / cnt - u_pool[:, cols]
        y = jnp.dot(pooled.astype(BF16), pool_w[g], preferred_element_type=F32)
        ycat_ref[:, cols] = (y * pool_scale[:, cols]).astype(ycat_ref.dtype)
    new_pool = ext_pool[:, pl.ds(LV, POOL_TILE), :]
    ext_pool[:, 0:POOL_TILE, :] = new_pool
    tick()

    ext_sconv[:, CONV_TILE:, :] = x[:, C_XBC:C_XBC + SSD_CONV_DIM].reshape(S, T, SSD_CONV_DIM)
    conv_blocks = []
    for j, n in _col_chunks(SSD_CONV_DIM):
        conv = sconv_b[:, j:j + n].reshape(1, 1, n)
        for k in range(CONV_WIDTH):
            conv = conv + sconv_w[k:k + 1, j:j + n].reshape(1, 1, n) * ext_sconv[:, pl.ds(CONV_TILE - 3 + k, T),
                                                                                  j:j + n]
        conv_blocks.append(_silu(conv).reshape(R, n))
        tick()
    new_sconv = ext_sconv[:, pl.ds(LV, CONV_TILE), :]
    ext_sconv[:, 0:CONV_TILE, :] = new_sconv
    xbc = jnp.concatenate(conv_blocks, axis=1)

    lane = lax.broadcasted_iota(jnp.int32, (1, LANES), 1)
    dt = _softplus(x[:, C_DT:C_DT + LANES] + dt_bias[...])
    dt = jnp.where((lane < SSD_HEADS) & (t_in < LV), dt, 0.0)
    d_a = dt * (-jnp.exp(a_log[...]))
    z = x[:, C_Z:C_Z + SSD_WIDTH]

    li = lax.broadcasted_iota(jnp.int32, (SUBCHUNK, SUBCHUNK), 0)
    si = lax.broadcasted_iota(jnp.int32, (SUBCHUNK, SUBCHUNK), 1)
    same_seq = (li // T) == (si // T) if S > 1 else (li >= 0)
    causal = same_seq & (si <= li)
    causal_b = causal.astype(BF16)
    same_b = same_seq.astype(BF16)
    lane_q = lax.broadcasted_iota(jnp.int32, (SUBCHUNK, LANES), 1)
    row_q = lax.broadcasted_iota(jnp.int32, (SUBCHUNK, 1), 0)
    expand_m = expand[...]

    for q in range(R // SUBCHUNK):
        rs = slice(q * SUBCHUNK, (q + 1) * SUBCHUNK)
        xs = xbc[rs, 0:SSD_WIDTH]
        d_a_q = d_a[rs]
        acs = _sel_dot(causal_b, d_a_q)
        tot = _sel_dot(same_b, d_a_q)
        acs_row = acs.T
        dt_e = _spread(dt[rs], expand_m)
        eacs_e = _spread(jnp.exp(acs), expand_m)
        etot = jnp.exp(tot)
        etot_e = _spread(etot if S > 1 else etot[0:SUBLANES], expand_m)
        xdt = xs * dt_e
        xdt_b = xdt.astype(BF16)
        xdecay_b = (xdt * _spread(jnp.exp(tot - acs), expand_m)).astype(BF16)
        tick()
        y_parts = []
        for g in range(SSD_GROUPS):
            gc = slice(g * SSD_GWIDTH, (g + 1) * SSD_GWIDTH)
            b_g = xbc[rs, SSD_WIDTH + g * SSD_STATE:SSD_WIDTH + (g + 1) * SSD_STATE].astype(BF16)
            c_g = xbc[rs, SSD_WIDTH + (SSD_GROUPS + g) * SSD_STATE:
                      SSD_WIDTH + (SSD_GROUPS + g + 1) * SSD_STATE].astype(BF16)
            cb = lax.dot_general(c_g, b_g, (((1,), (1,)), ((), ())), preferred_element_type=F32)
            diag = []
            for j in range(SSD_GWIDTH // LANES):
                pair = []
                for hh in range(2):
                    k = g * (SSD_HEADS // SSD_GROUPS) + 2 * j + hh
                    seg = acs[:, k:k + 1] - acs_row[k:k + 1, :]
                    lmat = jnp.where(causal, jnp.exp(jnp.where(causal, seg, 0.0)), 0.0)
                    pair.append((cb * lmat).astype(BF16))
                xp = xdt_b[:, g * SSD_GWIDTH + j * LANES:g * SSD_GWIDTH + (j + 1) * LANES]
                top = jnp.where(lane_q < SSD_HEAD_DIM, xp, jnp.zeros_like(xp))
                bot = jnp.where(lane_q >= SSD_HEAD_DIM, xp, jnp.zeros_like(xp))
                diag.append(jnp.dot(jnp.concatenate(pair, axis=1), jnp.concatenate([top, bot], axis=0),
                                    preferred_element_type=F32))
            y_g = jnp.concatenate(diag, axis=1)
            if carried:
                h_t = h_state[0, g]
                y_g = y_g + jnp.dot(c_g, h_t.astype(BF16), preferred_element_type=F32) * eacs_e[:, gc]
                upd = lax.dot_general(b_g, xdecay_b[:, gc], (((0,), (0,)), ((), ())),
                                      preferred_element_type=F32)
                h_state[0, g] = h_t * etot_e[0:1, gc] + upd
            else:
                y_off = jnp.zeros((SUBCHUNK, SSD_GWIDTH), F32)
                for s in range(S):
                    mine = (row_q // T) == s
                    h_t = ssd_st[s, gc, :].T
                    c_s = jnp.where(mine, c_g, jnp.zeros_like(c_g))
                    b_s = jnp.where(mine, b_g, jnp.zeros_like(b_g))
                    y_off = y_off + jnp.dot(c_s, h_t.astype(BF16), preferred_element_type=F32)
                    upd = lax.dot_general(b_s, xdecay_b[:, gc], (((0,), (0,)), ((), ())),
                                          preferred_element_type=F32)
                    ssd_out[s, gc, :] = (h_t * etot_e[s * T:s * T + 1, gc] + upd).T
                y_g = y_g + y_off * eacs_e[:, gc]
            y_parts.append(y_g)
            tick()
        y = jnp.concatenate(y_parts, axis=1) + xs * d_skip[...]
        y = _rms(y * _silu(z[rs]), ssd_norm[...])
        ycat_ref[rs, POOL_WIDTH:POOL_WIDTH + SSD_WIDTH] = y.astype(ycat_ref.dtype)

    ext_lconv[:, CONV_TILE:, :] = x[:, C_LRU:C_LRU + LRU_WIDTH].reshape(S, T, LRU_WIDTH)
    xc = lconv_b[...].reshape(1, 1, LRU_WIDTH)
    for k in range(CONV_WIDTH):
        xc = xc + lconv_w[k:k + 1, :].reshape(1, 1, LRU_WIDTH) * ext_lconv[:, pl.ds(CONV_TILE - 3 + k, T), :]
    new_lconv = ext_lconv[:, pl.ds(LV, CONV_TILE), :]
    ext_lconv[:, 0:CONV_TILE, :] = new_lconv
    xc = xc.reshape(R, LRU_WIDTH)
    tick()
    gates = jnp.dot(xc.astype(BF16), gate_w[...], preferred_element_type=F32) + gate_b[...]
    r_gate = jax.nn.sigmoid(gates[:, 0:LRU_WIDTH])
    i_gate = jax.nn.sigmoid(gates[:, LRU_WIDTH:2 * LRU_WIDTH])
    log_a = (-LRU_C) * r_gate * _softplus(-lam[...])
    a = jnp.exp(log_a)
    mult = jnp.where(pos == 0, 1.0, jnp.sqrt(1.0 - jnp.exp(2.0 * log_a)))
    b = mult * i_gate * xc
    tick()
    tiles = R // SUBLANES
    a3 = a.reshape(tiles, SUBLANES, LRU_WIDTH)
    b3 = b.reshape(tiles, SUBLANES, LRU_WIDTH)
    sub = lax.broadcasted_iota(jnp.int32, (1, SUBLANES, 1), 1)
    d = 1
    while d < SUBLANES:
        keep = sub >= d
        a_prev = jnp.where(keep, pltpu.roll(a3, d, 1), 1.0)
        b_prev = jnp.where(keep, pltpu.roll(b3, d, 1), 0.0)
        b3 = a3 * b_prev + b3
        a3 = a3 * a_prev
        d *= 2
    if S == 1:
        h_prev = lru_carry[0, 0:1, :]
        h_tiles = []
        for i in range(tiles):
            h_i = a3[i] * h_prev + b3[i]
            h_tiles.append(h_i)
            h_prev = h_i[SUBLANES - 1:SUBLANES, :]
        h = jnp.concatenate(h_tiles, axis=0)
    else:
        h = (a3 * lru_carry[...] + b3).reshape(R, LRU_WIDTH)
    gate = x[:, C_GATE:C_GATE + LRU_WIDTH]
    ycat_ref[:, POOL_WIDTH + SSD_WIDTH:D_MIX] = (h * _gelu_tanh(gate)).astype(ycat_ref.dtype)
    if S == 1:
        lru_carry[0] = jnp.broadcast_to(h[LV - 1:LV, :], (SUBLANES, LRU_WIDTH))
    else:
        li_r = lax.broadcasted_iota(jnp.int32, (R, R), 0)
        si_r = lax.broadcasted_iota(jnp.int32, (R, R), 1)
        pick_last = ((li_r // T) == (si_r // T)) & ((si_r % T) == LV - 1)
        lru_carry[...] = _sel_dot(pick_last.astype(BF16), h).reshape(S, T, LRU_WIDTH)
    for task in side_work:
        task()


def _mix_kernel(*refs, S, T, NC, LV, pos0, fused):
    refs = list(refs)
    if fused:
        h_ref, h_next, norm_g, w_in, w_out = refs[:5]
        del refs[:5]
    else:
        x = refs.pop(0)
    io = tuple(refs[:31])
    (pool_st, sconv_st, ssd_st, lconv_st, lru_st) = io[:5]
    (out_ref, pool_out, sconv_out, ssd_out, lconv_out, lru_out,
     ext_pool, ext_sconv, ext_lconv, lru_carry, h_state) = io[20:]
    cfg = dict(S=S, T=T, NC=NC, LV=LV, pos0=pos0)
    c = pl.program_id(1)
    carried = NC > 1

    @pl.when(c == 0)
    def _load_state():
        ext_pool[:, 0:POOL_TILE, :] = pool_st[...]
        ext_sconv[:, 0:CONV_TILE, :] = sconv_st[...]
        ext_lconv[:, 0:CONV_TILE, :] = lconv_st[...]
        lru_carry[...] = lru_st[...]
        if carried:
            for g in range(SSD_GROUPS):
                h_state[0, g] = ssd_st[0, g * SSD_GWIDTH:(g + 1) * SSD_GWIDTH, :].T

    if not fused:
        _mix_chunk(x, out_ref, c, io, **cfg)
    else:
        proj_even, proj_odd, ycat_ref = refs[31:34]

        def projection_steps(src, dst):
            normed = []

            def column_step(j, n):
                if not normed:
                    normed.append(_rms(src[...], norm_g[...]).astype(BF16))
                dst[:, j:j + n] = jnp.dot(normed[0], w_in[:, j:j + n], preferred_element_type=F32)

            return [functools.partial(column_step, j, n) for j, n in _col_chunks(N_PROJ)]

        @pl.when(c == 0)
        def _first_projection():
            for step in projection_steps(h_ref, proj_even):
                step()

        for parity, cur, nxt in ((0, proj_even, proj_odd), (1, proj_odd, proj_even)):
            @pl.when(c % 2 == parity)
            def _chunk(cur=cur, nxt=nxt):
                _mix_chunk(cur, ycat_ref, c, io, side_work=projection_steps(h_next, nxt), **cfg)
                mixed = ycat_ref[...]
                for j, n in _col_chunks(D_MODEL):
                    out_ref[:, j:j + n] = h_ref[:, j:j + n] + jnp.dot(mixed, w_out[:, j:j + n],
                                                                      preferred_element_type=F32)

    @pl.when(c == NC - 1)
    def _store_state():
        pool_out[...] = ext_pool[:, 0:POOL_TILE, :]
        sconv_out[...] = ext_sconv[:, 0:CONV_TILE, :]
        lconv_out[...] = ext_lconv[:, 0:CONV_TILE, :]
        lru_out[...] = lru_carry[...]
        if carried:
            for g in range(SSD_GROUPS):
                ssd_out[0, g * SSD_GWIDTH:(g + 1) * SSD_GWIDTH, :] = h_state[0, g].T


def _mix(x, st, ssd_state, ssd_layer, ssd_new, out_layer, w, *, S, T, NC, LV, pos0, proj=None):
    rows = x.shape[0]
    nseq = rows // (NC * T)
    R = S * T
    fused = proj is not None
    assert nseq % S == 0 and R % SUBCHUNK == 0
    assert (S == 1) or (NC == 1 and R == SUBCHUNK and T == SUBLANES)
    seq3 = lambda i, c: (i, 0, 0)
    row_blk = lambda i, c: (i * NC + c, 0)
    if fused:
        next_blk = lambda i, c: (i * NC + jnp.minimum(c + 1, NC - 1), 0)
        lead_specs = [pl.BlockSpec((R, D_MODEL), row_blk), pl.BlockSpec((R, D_MODEL), next_blk),
                      _resident((1, D_MODEL)), _resident(proj[1].shape), _resident(proj[2].shape)]
        lead_args = [x, x, proj[0].reshape(1, D_MODEL), proj[1], proj[2]]
    else:
        lead_specs = [pl.BlockSpec((R, N_PROJ), row_blk)]
        lead_args = [x]
    in_specs = lead_specs + [
        pl.BlockSpec((S, POOL_TILE, POOL_WIDTH), seq3),
        pl.BlockSpec((S, CONV_TILE, SSD_CONV_DIM), seq3),
        pl.BlockSpec((None, S, SSD_WIDTH, SSD_STATE), lambda i, c: (ssd_layer, i, 0, 0)),
        pl.BlockSpec((S, CONV_TILE, LRU_WIDTH), seq3),
        pl.BlockSpec((S, SUBLANES, LRU_WIDTH), seq3),
    ] + [_resident(a.shape) for a in w] + [pl.BlockSpec(memory_space=pl.ANY)]
    n_in = len(in_specs)
    if ssd_new is None:
        ssd_new, aliases = jnp.zeros((SUBLANES, LANES), F32), {}
    else:
        aliases = {n_in - 1: 3}
    out_specs = [
        pl.BlockSpec((R, D_MODEL if fused else D_MIX), row_blk),
        pl.BlockSpec((S, POOL_TILE, POOL_WIDTH), seq3),
        pl.BlockSpec((S, CONV_TILE, SSD_CONV_DIM), seq3),
        pl.BlockSpec((None, S, SSD_WIDTH, SSD_STATE), lambda i, c: (out_layer, i, 0, 0)),
        pl.BlockSpec((S, CONV_TILE, LRU_WIDTH), seq3),
        pl.BlockSpec((S, SUBLANES, LRU_WIDTH), seq3),
    ]
    out_shape = [
        jax.ShapeDtypeStruct((rows, D_MODEL), F32) if fused else jax.ShapeDtypeStruct((rows, D_MIX), BF16),
        jax.ShapeDtypeStruct((nseq, POOL_TILE, POOL_WIDTH), F32),
        jax.ShapeDtypeStruct((nseq, CONV_TILE, SSD_CONV_DIM), F32),
        jax.ShapeDtypeStruct((DEPTH, nseq, SSD_WIDTH, SSD_STATE), F32),
        jax.ShapeDtypeStruct((nseq, CONV_TILE, LRU_WIDTH), F32),
        jax.ShapeDtypeStruct((nseq, SUBLANES, LRU_WIDTH), F32),
    ]
    h_scratch = (1, SSD_GROUPS, SSD_STATE, SSD_GWIDTH) if NC > 1 else (1, 1, SUBLANES, LANES)
    scratch = [
        pltpu.VMEM((S, POOL_TILE + T, POOL_WIDTH), F32),
        pltpu.VMEM((S, CONV_TILE + T, SSD_CONV_DIM), F32),
        pltpu.VMEM((S, CONV_TILE + T, LRU_WIDTH), F32),
        pltpu.VMEM((S, SUBLANES, LRU_WIDTH), F32),
        pltpu.VMEM(h_scratch, F32),
    ]
    if fused:
        scratch += [pltpu.VMEM((R, N_PROJ), F32), pltpu.VMEM((R, N_PROJ), F32), pltpu.VMEM((R, D_MIX), BF16)]
    return pl.pallas_call(
        functools.partial(_mix_kernel, S=S, T=T, NC=NC, LV=LV, pos0=pos0, fused=fused),
        grid=(nseq // S, NC),
        in_specs=in_specs,
        out_specs=out_specs,
        out_shape=out_shape,
        scratch_shapes=scratch,
        input_output_aliases=aliases,
        name="mix_seq%d" % S,
        compiler_params=_cparams(("parallel", "arbitrary")),
    )(*lead_args, st[0], st[1], ssd_state, st[2], st[3], *w, ssd_new)


def _expand_matrix():
    e = np.zeros((LANES, SSD_WIDTH), np.float32)
    for k in range(SSD_HEADS):
        e[k, k * SSD_HEAD_DIM:(k + 1) * SSD_HEAD_DIM] = 1.0
    return jnp.asarray(np.concatenate([e, e], axis=0), BF16)


def _block_diag(w):
    eye = jnp.eye(LRU_HEADS, dtype=w.dtype)
    return jnp.einsum('hij,hg->higj', w, eye).reshape(LRU_WIDTH, LRU_WIDTH)


def _pad_lanes(v):
    return jnp.pad(v, (0, LANES - v.shape[0])).reshape(1, LANES)


def _layer_params(l, P):
    w_in = P['w_in'][l]
    off_dt = 3072
    w_in = jnp.concatenate(
        [w_in[:, :off_dt], w_in[:, off_dt + SSD_HEADS:], w_in[:, off_dt:off_dt + SSD_HEADS],
         jnp.zeros((D_MODEL, N_PROJ - C_DT - SSD_HEADS), F32)], axis=1).astype(BF16)
    mix_w = (
        P['pool_w'][l].astype(BF16),
        P['pool_scale'][l].reshape(1, POOL_WIDTH),
        P['ssd_conv_w'][l],
        P['ssd_conv_b'][l].reshape(1, SSD_CONV_DIM),
        _pad_lanes(P['ssd_dt_bias'][l]),
        _pad_lanes(P['ssd_a_log'][l]),
        jnp.repeat(P['ssd_d'][l], SSD_HEAD_DIM).reshape(1, SSD_WIDTH),
        P['ssd_norm'][l].reshape(1, SSD_WIDTH),
        P['lru_conv_w'][l],
        P['lru_conv_b'][l].reshape(1, LRU_WIDTH),
        jnp.concatenate([_block_diag(P['lru_wa'][l]), _block_diag(P['lru_wx'][l])], axis=1).astype(BF16),
        jnp.concatenate([P['lru_ba'][l].reshape(1, LRU_WIDTH), P['lru_bx'][l].reshape(1, LRU_WIDTH)], axis=1),
        P['lru_lambda'][l].reshape(1, LRU_WIDTH),
        _expand_matrix(),
    )
    return dict(
        w_in=w_in, mix=mix_w, norm_mix=P['norm_mix'][l],
        w_out=P['w_out'][l].astype(BF16),
        norm_mem=P['norm_mem'][l], wq=P['w_mem_q'][l].astype(BF16), wo=P['w_mem_o'][l].astype(BF16),
        norm_ffn=P['norm_ffn'][l], wg=P['w_ffn_gate'][l].astype(BF16), wu=P['w_ffn_up'][l].astype(BF16),
        wd=P['w_ffn_down'][l].astype(BF16),
    )


def _run_trunk(h, layers, norm_final, states, ssd_state, attend, *, mix_cfg, fuse_proj):
    new_states = []
    ssd_new = None
    for l, lp in enumerate(layers):
        ssd_layer = min(l, ssd_state.shape[0] - 1)
        if fuse_proj:
            h, *st = _mix(h, states[l], ssd_state, ssd_layer, ssd_new, l, lp['mix'],
                          proj=(lp['norm_mix'], lp['w_in'], lp['w_out']), **mix_cfg)
        else:
            proj = _linear(h, lp['w_in'], name="in_proj", gain=lp['norm_mix'])
            ycat, *st = _mix(proj, states[l], ssd_state, ssd_layer, ssd_new, l, lp['mix'], **mix_cfg)
            h = _linear(ycat, lp['w_out'], name="out_proj", residual=h)
        ssd_new = st[2]
        h = attend(h, lp, l)
        h = _ffn(h, lp['norm_ffn'], lp['wg'], lp['wu'], lp['wd'])
        new_states.append(st)
    return _final_norm(h, norm_final), new_states


def _unpack_states(new_states, nseq):
    pool = jnp.stack([s[0][:, 1:POOL_TILE] for s in new_states])
    sconv = jnp.stack([s[1][:, CONV_TILE - 3:CONV_TILE] for s in new_states])
    ssd = new_states[-1][2].reshape(DEPTH, nseq, SSD_HEADS, SSD_HEAD_DIM, SSD_STATE)
    lconv = jnp.stack([s[3][:, CONV_TILE - 3:CONV_TILE] for s in new_states])
    lru = jnp.stack([s[4][:, 0] for s in new_states])
    return pool, sconv, ssd, lconv, lru


def _trunk_prompt(x_prompt, mem_prompt, P, layers, *, chunk=256, tq=512):
    batch, seq, _ = x_prompt.shape
    chunk = min(chunk, seq)
    tq = min(tq, seq)
    wk = P['w_mem_k'].astype(BF16)
    wv = P['w_mem_v'].astype(BF16)
    mem_k, mem_v = _kv_proj(mem_prompt.reshape(batch * N_MEM, D_MODEL), wk, wv)
    mem_k = mem_k.reshape(DEPTH, batch, N_MEM, D_MODEL)
    mem_v = mem_v.reshape(DEPTH, batch, N_MEM, D_MODEL)
    zeros = (jnp.zeros((batch, POOL_TILE, POOL_WIDTH), F32), jnp.zeros((batch, CONV_TILE, SSD_CONV_DIM), F32),
             jnp.zeros((batch, CONV_TILE, LRU_WIDTH), F32), jnp.zeros((batch, SUBLANES, LRU_WIDTH), F32))
    ssd0 = jnp.zeros((1, batch, SSD_WIDTH, SSD_STATE), F32)
    y, new_states = _run_trunk(
        x_prompt.reshape(batch * seq, D_MODEL), layers, P['norm_final'], [zeros] * DEPTH, ssd0,
        lambda h, lp, l: _attn_block(h, lp['norm_mem'], lp['wq'], lp['wo'], mem_k, mem_v, l, tq=tq, seq_rows=seq),
        mix_cfg=dict(S=1, T=chunk, NC=seq // chunk, LV=chunk, pos0=0), fuse_proj=True)
    return (y.reshape(batch, seq, D_MODEL),) + _unpack_states(new_states, batch) + (
        mem_k.reshape(DEPTH, batch, N_MEM, MEM_HEADS, MEM_HDIM),
        mem_v.reshape(DEPTH, batch, N_MEM, MEM_HEADS, MEM_HDIM))


def _trunk_sample(x_sample, past_len, state_pool, state_ssd_conv, state_ssd, state_lru_conv, state_lru,
                  cache_k, cache_v, P, layers, *, seq_blk=16, attn_blk=8):
    nseq, seq, _ = x_sample.shape
    assert seq <= SUBLANES
    pad_t = SUBLANES - seq
    h = jnp.pad(x_sample, ((0, 0), (0, pad_t), (0, 0))).reshape(nseq * SUBLANES, D_MODEL)
    states = []
    for l in range(DEPTH):
        states.append((
            jnp.pad(state_pool[l], ((0, 0), (POOL_TILE - POOL_BUF, 0), (0, 0))),
            jnp.pad(state_ssd_conv[l], ((0, 0), (CONV_TILE - 3, 0), (0, 0))),
            jnp.pad(state_lru_conv[l], ((0, 0), (CONV_TILE - 3, 0), (0, 0))),
            jnp.broadcast_to(state_lru[l][:, None, :], (nseq, SUBLANES, LRU_WIDTH)),
        ))
    kc, vc = _cache_view(cache_k), _cache_view(cache_v)

    def attend(h, lp, l):
        q = _linear(h, lp['wq'], name="q_proj", gain=lp['norm_mem'])
        return _linear(_cache_attn(q, kc, vc, l, nseq_blk=attn_blk), lp['wo'], name="o_proj", residual=h)

    y, new_states = _run_trunk(
        h, layers, P['norm_final'], states, state_ssd.reshape(DEPTH, nseq, SSD_WIDTH, SSD_STATE), attend,
        mix_cfg=dict(S=seq_blk, T=SUBLANES, NC=1, LV=seq, pos0=past_len), fuse_proj=False)
    y = y.reshape(nseq, SUBLANES, D_MODEL)[:, :seq]
    return (y,) + _unpack_states(new_states, nseq)


def kernel(x_prompt, x_sample, mem_prompt, state_pool, state_ssd_conv, state_ssd, state_lru_conv, state_lru,
           cache_mem_k, cache_mem_v, norm_mix, w_in, pool_w, pool_scale, ssd_conv_w, ssd_conv_b, ssd_dt_bias,
           ssd_a_log, ssd_d, ssd_norm, lru_conv_w, lru_conv_b, lru_wa, lru_ba, lru_wx, lru_bx, lru_lambda,
           w_out, norm_mem, w_mem_q, w_mem_k, w_mem_v, w_mem_o, norm_ffn, w_ffn_gate, w_ffn_up, w_ffn_down,
           norm_final):
    P = dict(norm_mix=norm_mix, w_in=w_in, pool_w=pool_w, pool_scale=pool_scale, ssd_conv_w=ssd_conv_w,
             ssd_conv_b=ssd_conv_b, ssd_dt_bias=ssd_dt_bias, ssd_a_log=ssd_a_log, ssd_d=ssd_d, ssd_norm=ssd_norm,
             lru_conv_w=lru_conv_w, lru_conv_b=lru_conv_b, lru_wa=lru_wa, lru_ba=lru_ba, lru_wx=lru_wx,
             lru_bx=lru_bx, lru_lambda=lru_lambda, w_out=w_out, norm_mem=norm_mem, w_mem_q=w_mem_q,
             w_mem_k=w_mem_k, w_mem_v=w_mem_v, w_mem_o=w_mem_o, norm_ffn=norm_ffn, w_ffn_gate=w_ffn_gate,
             w_ffn_up=w_ffn_up, w_ffn_down=w_ffn_down, norm_final=norm_final)
    layers = [_layer_params(l, P) for l in range(DEPTH)]
    (y_prompt, p_pool, p_sconv, p_ssd, p_lconv, p_lru, p_mem_k, p_mem_v) = _trunk_prompt(
        x_prompt, mem_prompt, P, layers)
    (y_sample, s_pool, s_sconv, s_ssd, s_lconv, s_lru) = _trunk_sample(
        x_sample, PAST_LEN, state_pool, state_ssd_conv, state_ssd, state_lru_conv, state_lru,
        cache_mem_k, cache_mem_v, P, layers)
    return (y_prompt, y_sample, p_pool, p_sconv, p_ssd, p_lconv, p_lru, p_mem_k, p_mem_v,
            s_pool, s_sconv, s_ssd, s_lconv, s_lru)
```

```python
import functools
import itertools
import math

import numpy as np
import jax
import jax.numpy as jnp
from jax import lax
from jax.experimental import pallas as pl
from jax.experimental.pallas import tpu as pltpu

F32 = jnp.float32
BF16 = jnp.bfloat16

D_MODEL = 1024
DEPTH = 4
EPS = 1e-6
POOL_WIDTH = 512
POOL_WINDOWS = (2, 4, 8, 16)
POOL_GDIM = 128
POOL_BUF = 15
SSD_WIDTH = 1024
SSD_HEAD_DIM = 64
SSD_HEADS = 16
SSD_GROUPS = 2
SSD_GWIDTH = SSD_WIDTH // SSD_GROUPS
SSD_STATE = 128
SSD_CONV_DIM = 1536
CONV_WIDTH = 4
LRU_WIDTH = 512
LRU_HEADS = 8
LRU_HDIM = 64
LRU_C = 8.0
N_MEM = 256
MEM_HEADS = 4
MEM_HDIM = 256
D_FF = 2816
D_MIX = 2048
PAST_LEN = 16384

C_POOL = 0
C_Z = 512
C_XBC = 1536
C_GATE = 3072
C_LRU = 3584
C_DT = 4096
N_PROJ = 4224

LANES = 128
SUBLANES = 8
SUBCHUNK = 128
CONV_TILE = SUBLANES
POOL_TILE = 2 * SUBLANES
VMEM_LIMIT = 56 * 1024 * 1024


def _cparams(sem):
    return pltpu.CompilerParams(dimension_semantics=sem, vmem_limit_bytes=VMEM_LIMIT)


def _resident(shape):
    nd = len(shape)
    return pl.BlockSpec(shape, lambda *_: (0,) * nd, pipeline_mode=pl.Buffered(1))


def _rms(x, g):
    var = jnp.mean(x * x, axis=-1, keepdims=True)
    return x * lax.rsqrt(var + EPS) * g


def _silu(x):
    return x * jax.nn.sigmoid(x)


def _softplus(x):
    return jnp.maximum(x, 0.0) + jnp.log1p(jnp.exp(-jnp.abs(x)))


def _gelu_tanh(x):
    c = math.sqrt(2.0 / math.pi)
    return x * (0.5 * (1.0 + jnp.tanh(c * (x + 0.044715 * (x * x * x)))))


def _split3(x):
    x1 = x.astype(BF16)
    r1 = x - x1.astype(F32)
    x2 = r1.astype(BF16)
    x3 = (r1 - x2.astype(F32)).astype(BF16)
    return x1, x2, x3


def _sel_dot(sel, x):
    x1, x2, x3 = _split3(x)
    d = functools.partial(jnp.dot, preferred_element_type=F32)
    return d(sel, x1) + d(sel, x2) + d(sel, x3)


def _spread(x, sel2):
    hi = x.astype(BF16)
    lo = (x - hi.astype(F32)).astype(BF16)
    return jnp.dot(jnp.concatenate([hi, lo], axis=1), sel2, preferred_element_type=F32)


def _col_chunks(n, step=512):
    return [(j, min(step, n - j)) for j in range(0, n, step)]


def _linear_kernel(*refs, has_gain, has_res):
    x_ref, w_ref = refs[0], refs[1]
    k = 2
    g_ref = r_ref = None
    if has_gain:
        g_ref = refs[k]; k += 1
    if has_res:
        r_ref = refs[k]; k += 1
    o_ref = refs[k]
    x = x_ref[...]
    if has_gain:
        x = _rms(x.astype(F32), g_ref[...])
    xb = x.astype(BF16)
    for j, n in _col_chunks(o_ref.shape[1]):
        y = jnp.dot(xb, w_ref[:, j:j + n], preferred_element_type=F32)
        if has_res:
            y = y + r_ref[:, j:j + n]
        o_ref[:, j:j + n] = y.astype(o_ref.dtype)


def _linear(x, w, *, name, gain=None, residual=None, out_dtype=F32, tm=512):
    rows, kdim = x.shape
    n = w.shape[1]
    tm = min(tm, rows)
    assert rows % tm == 0
    in_specs = [pl.BlockSpec((tm, kdim), lambda i: (i, 0)), _resident(w.shape)]
    args = [x, w]
    if gain is not None:
        in_specs.append(_resident((1, kdim)))
        args.append(gain.reshape(1, kdim))
    if residual is not None:
        in_specs.append(pl.BlockSpec((tm, n), lambda i: (i, 0)))
        args.append(residual)
    return pl.pallas_call(
        functools.partial(_linear_kernel, has_gain=gain is not None, has_res=residual is not None),
        grid=(rows // tm,),
        in_specs=in_specs,
        out_specs=pl.BlockSpec((tm, n), lambda i: (i, 0)),
        out_shape=jax.ShapeDtypeStruct((rows, n), out_dtype),
        name=name,
        compiler_params=_cparams(("parallel",)),
    )(*args)


def _kv_kernel(x_ref, wk_ref, wv_ref, k_ref, v_ref):
    xb = x_ref[...].astype(BF16)
    k_ref[...] = jnp.dot(xb, wk_ref[...], preferred_element_type=F32)
    v_ref[...] = jnp.dot(xb, wv_ref[...], preferred_element_type=F32)


def _kv_proj(mem2d, wk, wv, tm=512):
    rows = mem2d.shape[0]
    tm = min(tm, rows)
    wspec = pl.BlockSpec((None, D_MODEL, D_MODEL), lambda l, i: (l, 0, 0))
    ospec = pl.BlockSpec((None, tm, D_MODEL), lambda l, i: (l, i, 0))
    oshape = jax.ShapeDtypeStruct((DEPTH, rows, D_MODEL), F32)
    return pl.pallas_call(
        _kv_kernel,
        grid=(DEPTH, rows // tm),
        in_specs=[pl.BlockSpec((tm, D_MODEL), lambda l, i: (i, 0)), wspec, wspec],
        out_specs=[ospec, ospec],
        out_shape=[oshape, oshape],
        name="kv_proj",
        compiler_params=_cparams(("parallel", "parallel")),
    )(mem2d, wk, wv)


def _ffn_kernel(x_ref, g_ref, wg_ref, wu_ref, wd_ref, *rest):
    o_ref = rest[-1]
    x = x_ref[...]
    ub = _rms(x, g_ref[...]).astype(BF16)
    acc = x
    for j, n in _col_chunks(D_FF, 1408):
        gate = jnp.dot(ub, wg_ref[:, j:j + n], preferred_element_type=F32)
        up = jnp.dot(ub, wu_ref[:, j:j + n], preferred_element_type=F32)
        act = (_silu(gate) * up).astype(BF16)
        acc = acc + jnp.dot(act, wd_ref[j:j + n, :], preferred_element_type=F32)
    if len(rest) == 2:
        acc = _rms(acc, rest[0][...])
    o_ref[...] = acc


def _ffn(h, gain, wg, wu, wd, final_gain=None, tm=512):
    rows = h.shape[0]
    tm = min(tm, rows)
    row_spec = pl.BlockSpec((tm, D_MODEL), lambda i: (i, 0))
    in_specs = [row_spec, _resident((1, D_MODEL)), _resident(wg.shape), _resident(wu.shape), _resident(wd.shape)]
    args = [h, gain.reshape(1, D_MODEL), wg, wu, wd]
    if final_gain is not None:
        in_specs.append(_resident((1, D_MODEL)))
        args.append(final_gain.reshape(1, D_MODEL))
    return pl.pallas_call(
        _ffn_kernel,
        grid=(rows // tm,),
        in_specs=in_specs,
        out_specs=row_spec,
        out_shape=jax.ShapeDtypeStruct((rows, D_MODEL), F32),
        name="ffn",
        compiler_params=_cparams(("parallel",)),
    )(*args)


def _attn_block_kernel(h_ref, g_ref, wq_ref, wo_ref, k_ref, v_ref, o_ref, kb_ref, vb_ref):
    scale = MEM_HDIM ** -0.5

    @pl.when(pl.program_id(1) == 0)
    def _cast_memory():
        kb_ref[...] = k_ref[0].astype(BF16)
        vb_ref[...] = v_ref[0].astype(BF16)

    x = h_ref[...]
    q = jnp.dot(_rms(x, g_ref[...]).astype(BF16), wq_ref[...], preferred_element_type=F32).astype(BF16)
    heads = []
    for hd in range(MEM_HEADS):
        cols = slice(hd * MEM_HDIM, (hd + 1) * MEM_HDIM)
        sc = lax.dot_general(q[:, cols], kb_ref[:, cols], (((1,), (1,)), ((), ())),
                             preferred_element_type=F32) * scale
        e = jnp.exp(sc - jnp.max(sc, axis=-1, keepdims=True))
        p = e / jnp.sum(e, axis=-1, keepdims=True)
        heads.append(jnp.dot(p.astype(BF16), vb_ref[:, cols], preferred_element_type=F32).astype(BF16))
    o = jnp.concatenate(heads, axis=1)
    o_ref[...] = x + jnp.dot(o, wo_ref[...], preferred_element_type=F32)


def _attn_block(h, gain, wq, wo, k, v, layer, *, tq, seq_rows):
    rows = h.shape[0]
    nseq = rows // seq_rows
    qblocks = seq_rows // tq
    h_spec = pl.BlockSpec((tq, D_MODEL), lambda i, c: (i * qblocks + c, 0))
    kv_spec = pl.BlockSpec((None, 1, N_MEM, D_MODEL), lambda i, c: (layer, i, 0, 0))
    return pl.pallas_call(
        _attn_block_kernel,
        grid=(nseq, qblocks),
        in_specs=[h_spec, _resident((1, D_MODEL)), _resident(wq.shape), _resident(wo.shape), kv_spec, kv_spec],
        out_specs=h_spec,
        out_shape=jax.ShapeDtypeStruct((rows, D_MODEL), F32),
        scratch_shapes=[pltpu.VMEM((N_MEM, D_MODEL), BF16), pltpu.VMEM((N_MEM, D_MODEL), BF16)],
        name="attn_block",
        compiler_params=_cparams(("parallel", "arbitrary")),
    )(h, gain.reshape(1, D_MODEL), wq, wo, k, v)


KV_SUB = 2 * MEM_HEADS
KV_ROWS = N_MEM * KV_SUB


def _cache_view(c):
    d, n = c.shape[:2]
    c = c.reshape(d, n, N_MEM, MEM_HEADS, MEM_HDIM // LANES, LANES)
    return jnp.transpose(c, (0, 1, 2, 4, 3, 5)).reshape(d, n, KV_ROWS, LANES)


def _cache_attn_kernel(q_ref, k_ref, v_ref, o_ref, *, nseq):
    scale = MEM_HDIM ** -0.5
    nvreg = KV_ROWS // LANES
    col_j = lax.broadcasted_iota(jnp.int32, (1, KV_ROWS), 1) % KV_SUB
    real = col_j < MEM_HEADS
    ts = []
    for s in range(nseq):
        q = q_ref[s * SUBLANES:(s + 1) * SUBLANES, :]
        pieces = []
        for j in range(KV_SUB):
            blk = (j % MEM_HEADS) * 2 + j // MEM_HEADS
            pieces.append(q[:, blk * LANES:(blk + 1) * LANES])
        qm = jnp.concatenate(pieces, axis=0).astype(BF16)
        kb = k_ref[s].astype(BF16)
        sc = lax.dot_general(qm, kb, (((1,), (1,)), ((), ())), preferred_element_type=F32)
        t = sc[0:SUBLANES]
        for j in range(1, KV_SUB):
            t = jnp.where(col_j == j, sc[j * SUBLANES:(j + 1) * SUBLANES], t)
        ts.append(t)
    t = jnp.concatenate(ts, axis=0)
    t = (t + pltpu.roll(t, KV_ROWS - MEM_HEADS, 1)) * scale
    t = jnp.where(real, t, jnp.finfo(F32).min)
    m = t[:, 0:LANES]
    for i in range(1, nvreg):
        m = jnp.maximum(m, t[:, i * LANES:(i + 1) * LANES])
    d = KV_SUB
    while d < LANES:
        m = jnp.maximum(m, pltpu.roll(m, d, 1))
        d *= 2
    e = jnp.where(real, jnp.exp(t - jnp.concatenate([m] * nvreg, axis=1)), 0.0)
    z = e[:, 0:LANES]
    for i in range(1, nvreg):
        z = z + e[:, i * LANES:(i + 1) * LANES]
    d = KV_SUB
    while d < LANES:
        z = z + pltpu.roll(z, d, 1)
        d *= 2
    z = jnp.where(real[:, 0:LANES], z, 1.0)
    p = e / jnp.concatenate([z] * nvreg, axis=1)
    p = p + pltpu.roll(p, MEM_HEADS, 1)
    for s in range(nseq):
        p_s = p[s * SUBLANES:(s + 1) * SUBLANES]
        pm = jnp.concatenate([jnp.where(col_j == j, p_s, 0.0) for j in range(KV_SUB)], axis=0).astype(BF16)
        o = jnp.dot(pm, v_ref[s].astype(BF16), preferred_element_type=F32)
        for blk in range(KV_SUB):
            j = (blk % 2) * MEM_HEADS + blk // 2
            o_ref[s * SUBLANES:(s + 1) * SUBLANES, blk * LANES:(blk + 1) * LANES] = (
                o[j * SUBLANES:(j + 1) * SUBLANES, :].astype(o_ref.dtype))


def _cache_attn(q, k, v, layer, *, nseq_blk=4):
    rows = q.shape[0]
    nseq = rows // SUBLANES
    assert q.shape[1] == KV_SUB * LANES and nseq % nseq_blk == 0
    q_spec = pl.BlockSpec((nseq_blk * SUBLANES, D_MODEL), lambda i: (i, 0))
    kv_spec = pl.BlockSpec((None, nseq_blk, KV_ROWS, LANES), lambda i: (layer, i, 0, 0))
    return pl.pallas_call(
        functools.partial(_cache_attn_kernel, nseq=nseq_blk),
        grid=(nseq // nseq_blk,),
        in_specs=[q_spec, kv_spec, kv_spec],
        out_specs=q_spec,
        out_shape=jax.ShapeDtypeStruct((rows, D_MODEL), BF16),
        name="cache_attn",
        compiler_params=_cparams(("parallel",)),
    )(q, k, v)


def _causal_conv(xt, carry, w_ref, b_ref, cols, *, T):
    n = xt.shape[-1]
    sub = lax.broadcasted_iota(jnp.int32, (1, SUBLANES, 1), 1)
    y = b_ref[:, cols].reshape(1, 1, n) + w_ref[CONV_WIDTH - 1:CONV_WIDTH, cols].reshape(1, 1, n) * xt
    for d in range(1, CONV_WIDTH):
        cur = pltpu.roll(xt, d, 1)
        before = pltpu.roll(carry, d, 1)
        if T != SUBLANES:
            before = jnp.concatenate([before, cur[:-1]], axis=0)
        tap = w_ref[CONV_WIDTH - 1 - d:CONV_WIDTH - d, cols].reshape(1, 1, n)
        y = y + tap * jnp.where(sub < d, before, cur)
    return y


def _conv_carry(xt, *, T, LV):
    if T != SUBLANES:
        assert LV == T
        return xt[-1:]
    assert LV >= CONV_WIDTH - 1
    return xt if LV == SUBLANES else pltpu.roll(xt, SUBLANES - LV, 1)


def _mix_chunk(x, ycat_ref, c, refs, *, S, T, NC, LV, pos0, side_work=(), late_work=()):
    pending = [iter(side_work)]

    def tick():
        task = next(pending[0], None)
        if task is not None:
            task()

    (pool_st, sconv_st, ssd_st, lconv_st, lru_st,
     pool_w, pool_scale, sconv_w, sconv_b, dt_bias, a_log, d_skip, ssd_norm,
     lconv_w, lconv_b, gate_w, gate_b, lam, expand, ssd_prev,
     out_ref, pool_out, sconv_out, ssd_out, lconv_out, lru_out,
     ext_pool, ext_sconv, ext_lconv, lru_carry, h_state) = refs
    R = S * T
    carried = NC > 1
    row = lax.broadcasted_iota(jnp.int32, (R, 1), 0)
    t_in = row % T
    pos = pos0 + c * LV + t_in

    u_pool = x[:, C_POOL:C_POOL + POOL_WIDTH]
    ext_pool[:, POOL_TILE:, :] = u_pool.reshape(S, T, POOL_WIDTH)
    for g, w in enumerate(POOL_WINDOWS):
        cols = slice(g * POOL_GDIM, (g + 1) * POOL_GDIM)
        win = ext_pool[:, :, cols].reshape(S * (POOL_TILE + T), POOL_GDIM)
        shift = 1
        while shift < w:
            win = win + pltpu.roll(win, shift, 0)
            shift *= 2
        acc = win.reshape(S, POOL_TILE + T, POOL_GDIM)[:, POOL_TILE:, :]
        cnt = jnp.minimum(pos + 1, w).astype(F32)
        pooled = acc.reshape(R, POOL_GDIM) / cnt - u_pool[:, cols]
        y = jnp.dot(pooled.astype(BF16), pool_w[g], preferred_element_type=F32)
        ycat_ref[:, cols] = (y * pool_scale[:, cols]).astype(ycat_ref.dtype)
    new_pool = ext_pool[:, pl.ds(LV, POOL_TILE), :]
    ext_pool[:, 0:POOL_TILE, :] = new_pool

    conv_blocks = []
    for j, n in _col_chunks(SSD_CONV_DIM):
        cols = slice(j, j + n)
        xt = x[:, C_XBC + j:C_XBC + j + n].reshape(R // SUBLANES, SUBLANES, n)
        conv = _causal_conv(xt, ext_sconv[:, :, cols], sconv_w, sconv_b, cols, T=T)
        ext_sconv[:, :, cols] = _conv_carry(xt, T=T, LV=LV)
        conv_blocks.append(_silu(conv).reshape(R, n))
    xbc = jnp.concatenate(conv_blocks, axis=1)

    lane = lax.broadcasted_iota(jnp.int32, (1, LANES), 1)
    dt = _softplus(x[:, C_DT:C_DT + LANES] + dt_bias[...])
    dt = jnp.where((lane < SSD_HEADS) & (t_in < LV), dt, 0.0)
    d_a = dt * (-jnp.exp(a_log[...]))
    z = x[:, C_Z:C_Z + SSD_WIDTH]

    li = lax.broadcasted_iota(jnp.int32, (SUBCHUNK, SUBCHUNK), 0)
    si = lax.broadcasted_iota(jnp.int32, (SUBCHUNK, SUBCHUNK), 1)
    same_seq = (li // T) == (si // T) if S > 1 else (li >= 0)
    causal = same_seq & (si <= li)
    causal_b = causal.astype(BF16)
    same_b = same_seq.astype(BF16)
    lane_q = lax.broadcasted_iota(jnp.int32, (SUBCHUNK, LANES), 1)
    row_q = lax.broadcasted_iota(jnp.int32, (SUBCHUNK, 1), 0)
    expand_m = expand[...]

    for q in range(R // SUBCHUNK):
        rs = slice(q * SUBCHUNK, (q + 1) * SUBCHUNK)
        xs = xbc[rs, 0:SSD_WIDTH]
        d_a_q = d_a[rs]
        acs = _sel_dot(causal_b, d_a_q)
        tot = _sel_dot(same_b, d_a_q)
        acs_row = acs.T
        dt_e = _spread(dt[rs], expand_m)
        eacs_e = _spread(jnp.exp(acs), expand_m)
        etot = jnp.exp(tot)
        etot_e = _spread(etot[0:SUBLANES] if carried else etot, expand_m)
        xdt = xs * dt_e
        xdt_b = xdt.astype(BF16)
        xdecay = xdt * _spread(jnp.exp(tot - acs), expand_m)
        xdecay_b = xdecay.astype(BF16)
        tick()
        y_parts = []
        for g in range(SSD_GROUPS):
            gc = slice(g * SSD_GWIDTH, (g + 1) * SSD_GWIDTH)
            b_g = xbc[rs, SSD_WIDTH + g * SSD_STATE:SSD_WIDTH + (g + 1) * SSD_STATE].astype(BF16)
            c_g = xbc[rs, SSD_WIDTH + (SSD_GROUPS + g) * SSD_STATE:
                      SSD_WIDTH + (SSD_GROUPS + g + 1) * SSD_STATE].astype(BF16)
            cb = lax.dot_general(c_g, b_g, (((1,), (1,)), ((), ())), preferred_element_type=F32)
            diag = []
            for j in range(SSD_GWIDTH // LANES):
                pair = []
                for hh in range(2):
                    k = g * (SSD_HEADS // SSD_GROUPS) + 2 * j + hh
                    seg = acs[:, k:k + 1] - acs_row[k:k + 1, :]
                    lmat = jnp.where(causal, jnp.exp(jnp.where(causal, seg, 0.0)), 0.0)
                    pair.append((cb * lmat).astype(BF16))
                xp = xdt_b[:, g * SSD_GWIDTH + j * LANES:g * SSD_GWIDTH + (j + 1) * LANES]
                top = jnp.where(lane_q < SSD_HEAD_DIM, xp, jnp.zeros_like(xp))
                bot = jnp.where(lane_q >= SSD_HEAD_DIM, xp, jnp.zeros_like(xp))
                diag.append(jnp.dot(jnp.concatenate(pair, axis=1), jnp.concatenate([top, bot], axis=0),
                                    preferred_element_type=F32))
            y_g = jnp.concatenate(diag, axis=1)
            if carried:
                h_t = h_state[0, g]
                y_g = y_g + jnp.dot(c_g, h_t.astype(BF16), preferred_element_type=F32) * eacs_e[:, gc]
                upd = lax.dot_general(b_g, xdecay_b[:, gc], (((0,), (0,)), ((), ())),
                                      preferred_element_type=F32)
                h_state[0, g] = h_t * etot_e[0:1, gc] + upd
            else:
                xd_t = xdecay[:, gc].T.astype(BF16)
                etot_t = etot_e[:, gc].T
                y_off = jnp.zeros((SUBCHUNK, SSD_GWIDTH), F32)
                for s in range(S):
                    mine = (row_q // T) == s
                    h_s = ssd_st[s, gc, :]
                    c_s = jnp.where(mine, c_g, jnp.zeros_like(c_g))
                    b_s = jnp.where(mine, b_g, jnp.zeros_like(b_g))
                    y_off = y_off + lax.dot_general(c_s, h_s.astype(BF16), (((1,), (1,)), ((), ())),
                                                    preferred_element_type=F32)
                    upd = jnp.dot(xd_t, b_s, preferred_element_type=F32)
                    ssd_out[s, gc, :] = h_s * etot_t[:, s * T:s * T + 1] + upd
                y_g = y_g + y_off * eacs_e[:, gc]
            y_parts.append(y_g)
            tick()
        y = jnp.concatenate(y_parts, axis=1) + xs * d_skip[...]
        y = _rms(y * _silu(z[rs]), ssd_norm[...])
        ycat_ref[rs, POOL_WIDTH:POOL_WIDTH + SSD_WIDTH] = y.astype(ycat_ref.dtype)

    pending[0] = itertools.chain(pending[0], late_work)
    xt = x[:, C_LRU:C_LRU + LRU_WIDTH].reshape(R // SUBLANES, SUBLANES, LRU_WIDTH)
    xc = _causal_conv(xt, ext_lconv[...], lconv_w, lconv_b, slice(0, LRU_WIDTH), T=T).reshape(R, LRU_WIDTH)
    ext_lconv[...] = _conv_carry(xt, T=T, LV=LV)
    tick()
    gates = jnp.dot(xc.astype(BF16), gate_w[...], preferred_element_type=F32) + gate_b[...]
    r_gate = jax.nn.sigmoid(gates[:, 0:LRU_WIDTH])
    i_gate = jax.nn.sigmoid(gates[:, LRU_WIDTH:2 * LRU_WIDTH])
    log_a = (-LRU_C) * r_gate * _softplus(-lam[...])
    a = jnp.exp(log_a)
    mult = jnp.where(pos == 0, 1.0, jnp.sqrt(1.0 - jnp.exp(2.0 * log_a)))
    b = mult * i_gate * xc
    tick()
    tiles = R // SUBLANES
    a3 = a.reshape(tiles, SUBLANES, LRU_WIDTH)
    b3 = b.reshape(tiles, SUBLANES, LRU_WIDTH)
    sub = lax.broadcasted_iota(jnp.int32, (1, SUBLANES, 1), 1)
    d = 1
    while d < SUBLANES:
        keep = sub >= d
        a_prev = jnp.where(keep, pltpu.roll(a3, d, 1), 1.0)
        b_prev = jnp.where(keep, pltpu.roll(b3, d, 1), 0.0)
        b3 = a3 * b_prev + b3
        a3 = a3 * a_prev
        d *= 2
        tick()
    if S == 1:
        h_prev = lru_carry[0, 0:1, :]
        h_tiles = []
        for i in range(tiles):
            h_i = a3[i] * h_prev + b3[i]
            h_tiles.append(h_i)
            h_prev = h_i[SUBLANES - 1:SUBLANES, :]
        h = jnp.concatenate(h_tiles, axis=0)
    else:
        h = (a3 * lru_carry[...] + b3).reshape(R, LRU_WIDTH)
    gate = x[:, C_GATE:C_GATE + LRU_WIDTH]
    ycat_ref[:, POOL_WIDTH + SSD_WIDTH:D_MIX] = (h * _gelu_tanh(gate)).astype(ycat_ref.dtype)
    if S == 1:
        lru_carry[0] = jnp.broadcast_to(h[LV - 1:LV, :], (SUBLANES, LRU_WIDTH))
    else:
        li_r = lax.broadcasted_iota(jnp.int32, (R, R), 0)
        si_r = lax.broadcasted_iota(jnp.int32, (R, R), 1)
        pick_last = ((li_r // T) == (si_r // T)) & ((si_r % T) == LV - 1)
        lru_carry[...] = _sel_dot(pick_last.astype(BF16), h).reshape(S, T, LRU_WIDTH)
    for task in pending[0]:
        task()


def _mix_kernel(*refs, S, T, NC, LV, pos0, fused):
    refs = list(refs)
    if fused:
        h_ref, h_next, norm_g, w_in, w_out = refs[:5]
        del refs[:5]
    else:
        x = refs.pop(0)
    io = tuple(refs[:31])
    (pool_st, sconv_st, ssd_st, lconv_st, lru_st) = io[:5]
    (out_ref, pool_out, sconv_out, ssd_out, lconv_out, lru_out,
     ext_pool, ext_sconv, ext_lconv, lru_carry, h_state) = io[20:]
    cfg = dict(S=S, T=T, NC=NC, LV=LV, pos0=pos0)
    c = pl.program_id(1)
    carried = NC > 1

    @pl.when(c == 0)
    def _load_state():
        ext_pool[:, 0:POOL_TILE, :] = pool_st[...]
        ext_sconv[:, 0:CONV_TILE, :] = sconv_st[...]
        ext_lconv[:, 0:CONV_TILE, :] = lconv_st[...]
        lru_carry[...] = lru_st[...]
        if carried:
            for g in range(SSD_GROUPS):
                h_state[0, g] = ssd_st[0, g * SSD_GWIDTH:(g + 1) * SSD_GWIDTH, :].T

    if not fused:
        _mix_chunk(x, out_ref, c, io, **cfg)
    else:
        proj_even, proj_odd, ycat_ref = refs[31:34]

        def projection_steps(src, dst):
            normed = []

            def column_step(j, n):
                if not normed:
                    normed.append(_rms(src[...], norm_g[...]).astype(BF16))
                dst[:, j:j + n] = jnp.dot(normed[0], w_in[:, j:j + n], preferred_element_type=F32)

            return [functools.partial(column_step, j, n) for j, n in _col_chunks(N_PROJ)]

        @pl.when(c == 0)
        def _first_projection():
            for step in projection_steps(h_ref, proj_even):
                step()

        for parity, cur, nxt in ((0, proj_even, proj_odd), (1, proj_odd, proj_even)):
            @pl.when(c % 2 == parity)
            def _chunk(cur=cur, nxt=nxt):
                early = POOL_WIDTH + SSD_WIDTH

                def project_early_columns():
                    out_ref[...] = h_ref[...] + jnp.dot(ycat_ref[:, 0:early], w_out[0:early, :],
                                                        preferred_element_type=F32)

                _mix_chunk(cur, ycat_ref, c, io, side_work=projection_steps(h_next, nxt),
                           late_work=[project_early_columns], **cfg)
                out_ref[...] += jnp.dot(ycat_ref[:, early:D_MIX], w_out[early:D_MIX, :],
                                        preferred_element_type=F32)

    @pl.when(c == NC - 1)
    def _store_state():
        pool_out[...] = ext_pool[:, 0:POOL_TILE, :]
        sconv_out[...] = ext_sconv[:, 0:CONV_TILE, :]
        lconv_out[...] = ext_lconv[:, 0:CONV_TILE, :]
        lru_out[...] = lru_carry[...]
        if carried:
            for g in range(SSD_GROUPS):
                ssd_out[0, g * SSD_GWIDTH:(g + 1) * SSD_GWIDTH, :] = h_state[0, g].T


def _mix(x, st, ssd_state, ssd_layer, ssd_new, out_layer, w, *, S, T, NC, LV, pos0, proj=None):
    rows = x.shape[0]
    nseq = rows // (NC * T)
    R = S * T
    fused = proj is not None
    assert nseq % S == 0 and R % SUBCHUNK == 0
    assert (S == 1) or (NC == 1 and R == SUBCHUNK and T == SUBLANES)
    seq3 = lambda i, c: (i, 0, 0)
    row_blk = lambda i, c: (i * NC + c, 0)
    if fused:
        next_blk = lambda i, c: (i * NC + jnp.minimum(c + 1, NC - 1), 0)
        lead_specs = [pl.BlockSpec((R, D_MODEL), row_blk), pl.BlockSpec((R, D_MODEL), next_blk),
                      _resident((1, D_MODEL)), _resident(proj[1].shape), _resident(proj[2].shape)]
        lead_args = [x, x, proj[0].reshape(1, D_MODEL), proj[1], proj[2]]
    else:
        lead_specs = [pl.BlockSpec((R, N_PROJ), row_blk)]
        lead_args = [x]
    in_specs = lead_specs + [
        pl.BlockSpec((S, POOL_TILE, POOL_WIDTH), seq3),
        pl.BlockSpec((S, CONV_TILE, SSD_CONV_DIM), seq3),
        pl.BlockSpec((None, S, SSD_WIDTH, SSD_STATE), lambda i, c: (ssd_layer, i, 0, 0)),
        pl.BlockSpec((S, CONV_TILE, LRU_WIDTH), seq3),
        pl.BlockSpec((S, SUBLANES, LRU_WIDTH), seq3),
    ] + [_resident(a.shape) for a in w] + [pl.BlockSpec(memory_space=pl.ANY)]
    if ssd_new is None:
        ssd_new = jnp.zeros((DEPTH, nseq, SSD_WIDTH, SSD_STATE), F32)
    aliases = {len(in_specs) - 1: 3}
    out_specs = [
        pl.BlockSpec((R, D_MODEL if fused else D_MIX), row_blk),
        pl.BlockSpec((S, POOL_TILE, POOL_WIDTH), seq3),
        pl.BlockSpec((S, CONV_TILE, SSD_CONV_DIM), seq3),
        pl.BlockSpec((None, S, SSD_WIDTH, SSD_STATE), lambda i, c: (out_layer, i, 0, 0)),
        pl.BlockSpec((S, CONV_TILE, LRU_WIDTH), seq3),
        pl.BlockSpec((S, SUBLANES, LRU_WIDTH), seq3),
    ]
    out_shape = [
        jax.ShapeDtypeStruct((rows, D_MODEL), F32) if fused else jax.ShapeDtypeStruct((rows, D_MIX), BF16),
        jax.ShapeDtypeStruct((nseq, POOL_TILE, POOL_WIDTH), F32),
        jax.ShapeDtypeStruct((nseq, CONV_TILE, SSD_CONV_DIM), F32),
        jax.ShapeDtypeStruct((DEPTH, nseq, SSD_WIDTH, SSD_STATE), F32),
        jax.ShapeDtypeStruct((nseq, CONV_TILE, LRU_WIDTH), F32),
        jax.ShapeDtypeStruct((nseq, SUBLANES, LRU_WIDTH), F32),
    ]
    h_scratch = (1, SSD_GROUPS, SSD_STATE, SSD_GWIDTH) if NC > 1 else (1, 1, SUBLANES, LANES)
    scratch = [
        pltpu.VMEM((S, POOL_TILE + T, POOL_WIDTH), F32),
        pltpu.VMEM((S, CONV_TILE, SSD_CONV_DIM), F32),
        pltpu.VMEM((S, CONV_TILE, LRU_WIDTH), F32),
        pltpu.VMEM((S, SUBLANES, LRU_WIDTH), F32),
        pltpu.VMEM(h_scratch, F32),
    ]
    if fused:
        scratch += [pltpu.VMEM((R, N_PROJ), F32), pltpu.VMEM((R, N_PROJ), F32), pltpu.VMEM((R, D_MIX), BF16)]
    return pl.pallas_call(
        functools.partial(_mix_kernel, S=S, T=T, NC=NC, LV=LV, pos0=pos0, fused=fused),
        grid=(nseq // S, NC),
        in_specs=in_specs,
        out_specs=out_specs,
        out_shape=out_shape,
        scratch_shapes=scratch,
        input_output_aliases=aliases,
        name="mix_seq%d" % S,
        compiler_params=_cparams(("parallel", "arbitrary")),
    )(*lead_args, st[0], st[1], ssd_state, st[2], st[3], *w, ssd_new)


def _expand_matrix():
    e = np.zeros((LANES, SSD_WIDTH), np.float32)
    for k in range(SSD_HEADS):
        e[k, k * SSD_HEAD_DIM:(k + 1) * SSD_HEAD_DIM] = 1.0
    return jnp.asarray(np.concatenate([e, e], axis=0), BF16)


def _block_diag(w):
    eye = jnp.eye(LRU_HEADS, dtype=w.dtype)
    return jnp.einsum('hij,hg->higj', w, eye).reshape(LRU_WIDTH, LRU_WIDTH)


def _pad_lanes(v):
    return jnp.pad(v, (0, LANES - v.shape[0])).reshape(1, LANES)


def _layer_params(l, P):
    w_in = P['w_in'][l]
    off_dt = 3072
    w_in = jnp.concatenate(
        [w_in[:, :off_dt], w_in[:, off_dt + SSD_HEADS:], w_in[:, off_dt:off_dt + SSD_HEADS],
         jnp.zeros((D_MODEL, N_PROJ - C_DT - SSD_HEADS), F32)], axis=1).astype(BF16)
    mix_w = (
        P['pool_w'][l].astype(BF16),
        P['pool_scale'][l].reshape(1, POOL_WIDTH),
        P['ssd_conv_w'][l],
        P['ssd_conv_b'][l].reshape(1, SSD_CONV_DIM),
        _pad_lanes(P['ssd_dt_bias'][l]),
        _pad_lanes(P['ssd_a_log'][l]),
        jnp.repeat(P['ssd_d'][l], SSD_HEAD_DIM).reshape(1, SSD_WIDTH),
        P['ssd_norm'][l].reshape(1, SSD_WIDTH),
        P['lru_conv_w'][l],
        P['lru_conv_b'][l].reshape(1, LRU_WIDTH),
        jnp.concatenate([_block_diag(P['lru_wa'][l]), _block_diag(P['lru_wx'][l])], axis=1).astype(BF16),
        jnp.concatenate([P['lru_ba'][l].reshape(1, LRU_WIDTH), P['lru_bx'][l].reshape(1, LRU_WIDTH)], axis=1),
        P['lru_lambda'][l].reshape(1, LRU_WIDTH),
        _expand_matrix(),
    )
    return dict(
        w_in=w_in, mix=mix_w, norm_mix=P['norm_mix'][l],
        w_out=P['w_out'][l].astype(BF16),
        norm_mem=P['norm_mem'][l], wq=P['w_mem_q'][l].astype(BF16), wo=P['w_mem_o'][l].astype(BF16),
        norm_ffn=P['norm_ffn'][l], wg=P['w_ffn_gate'][l].astype(BF16), wu=P['w_ffn_up'][l].astype(BF16),
        wd=P['w_ffn_down'][l].astype(BF16),
    )


def _run_trunk(h, layers, norm_final, states, ssd_state, attend, *, mix_cfg, fuse_proj):
    new_states = []
    ssd_new = None
    for l, lp in enumerate(layers):
        ssd_layer = min(l, ssd_state.shape[0] - 1)
        if fuse_proj:
            h, *st = _mix(h, states[l], ssd_state, ssd_layer, ssd_new, l, lp['mix'],
                          proj=(lp['norm_mix'], lp['w_in'], lp['w_out']), **mix_cfg)
        else:
            proj = _linear(h, lp['w_in'], name="in_proj", gain=lp['norm_mix'])
            ycat, *st = _mix(proj, states[l], ssd_state, ssd_layer, ssd_new, l, lp['mix'], **mix_cfg)
            h = _linear(ycat, lp['w_out'], name="out_proj", residual=h)
        ssd_new = st[2]
        h = attend(h, lp, l)
        last = l == len(layers) - 1
        h = _ffn(h, lp['norm_ffn'], lp['wg'], lp['wu'], lp['wd'], final_gain=norm_final if last else None)
        new_states.append(st)
    return h, new_states


def _unpack_states(new_states, nseq):
    pool = jnp.stack([s[0][:, 1:POOL_TILE] for s in new_states])
    sconv = jnp.stack([s[1][:, CONV_TILE - 3:CONV_TILE] for s in new_states])
    ssd = new_states[-1][2].reshape(DEPTH, nseq, SSD_HEADS, SSD_HEAD_DIM, SSD_STATE)
    lconv = jnp.stack([s[3][:, CONV_TILE - 3:CONV_TILE] for s in new_states])
    lru = jnp.stack([s[4][:, 0] for s in new_states])
    return pool, sconv, ssd, lconv, lru


def _trunk_prompt(x_prompt, mem_prompt, P, layers, *, chunk=256, tq=512):
    batch, seq, _ = x_prompt.shape
    chunk = min(chunk, seq)
    tq = min(tq, seq)
    wk = P['w_mem_k'].astype(BF16)
    wv = P['w_mem_v'].astype(BF16)
    mem_k, mem_v = _kv_proj(mem_prompt.reshape(batch * N_MEM, D_MODEL), wk, wv)
    mem_k = mem_k.reshape(DEPTH, batch, N_MEM, D_MODEL)
    mem_v = mem_v.reshape(DEPTH, batch, N_MEM, D_MODEL)
    zeros = (jnp.zeros((batch, POOL_TILE, POOL_WIDTH), F32), jnp.zeros((batch, CONV_TILE, SSD_CONV_DIM), F32),
             jnp.zeros((batch, CONV_TILE, LRU_WIDTH), F32), jnp.zeros((batch, SUBLANES, LRU_WIDTH), F32))
    ssd0 = jnp.zeros((1, batch, SSD_WIDTH, SSD_STATE), F32)
    y, new_states = _run_trunk(
        x_prompt.reshape(batch * seq, D_MODEL), layers, P['norm_final'], [zeros] * DEPTH, ssd0,
        lambda h, lp, l: _attn_block(h, lp['norm_mem'], lp['wq'], lp['wo'], mem_k, mem_v, l, tq=tq, seq_rows=seq),
        mix_cfg=dict(S=1, T=chunk, NC=seq // chunk, LV=chunk, pos0=0), fuse_proj=True)
    return (y.reshape(batch, seq, D_MODEL),) + _unpack_states(new_states, batch) + (
        mem_k.reshape(DEPTH, batch, N_MEM, MEM_HEADS, MEM_HDIM),
        mem_v.reshape(DEPTH, batch, N_MEM, MEM_HEADS, MEM_HDIM))


def _trunk_sample(x_sample, past_len, state_pool, state_ssd_conv, state_ssd, state_lru_conv, state_lru,
                  cache_k, cache_v, P, layers, *, seq_blk=16, attn_blk=8):
    nseq, seq, _ = x_sample.shape
    assert seq <= SUBLANES
    pad_t = SUBLANES - seq
    h = jnp.pad(x_sample, ((0, 0), (0, pad_t), (0, 0))).reshape(nseq * SUBLANES, D_MODEL)
    states = []
    for l in range(DEPTH):
        states.append((
            jnp.pad(state_pool[l], ((0, 0), (POOL_TILE - POOL_BUF, 0), (0, 0))),
            jnp.pad(state_ssd_conv[l], ((0, 0), (CONV_TILE - 3, 0), (0, 0))),
            jnp.pad(state_lru_conv[l], ((0, 0), (CONV_TILE - 3, 0), (0, 0))),
            jnp.broadcast_to(state_lru[l][:, None, :], (nseq, SUBLANES, LRU_WIDTH)),
        ))
    kc, vc = _cache_view(cache_k), _cache_view(cache_v)

    def attend(h, lp, l):
        q = _linear(h, lp['wq'], name="q_proj", gain=lp['norm_mem'])
        return _linear(_cache_attn(q, kc, vc, l, nseq_blk=attn_blk), lp['wo'], name="o_proj", residual=h)

    y, new_states = _run_trunk(
        h, layers, P['norm_final'], states, state_ssd.reshape(DEPTH, nseq, SSD_WIDTH, SSD_STATE), attend,
        mix_cfg=dict(S=seq_blk, T=SUBLANES, NC=1, LV=seq, pos0=past_len), fuse_proj=False)
    y = y.reshape(nseq, SUBLANES, D_MODEL)[:, :seq]
    return (y,) + _unpack_states(new_states, nseq)


def kernel(x_prompt, x_sample, mem_prompt, state_pool, state_ssd_conv, state_ssd, state_lru_conv, state_lru,
           cache_mem_k, cache_mem_v, norm_mix, w_in, pool_w, pool_scale, ssd_conv_w, ssd_conv_b, ssd_dt_bias,
           ssd_a_log, ssd_d, ssd_norm, lru_conv_w, lru_conv_b, lru_wa, lru_ba, lru_wx, lru_bx, lru_lambda,
           w_out, norm_mem, w_mem_q, w_mem_k, w_mem_v, w_mem_o, norm_ffn, w_ffn_gate, w_ffn_up, w_ffn_down,
           norm_final):
    P = dict(norm_mix=norm_mix, w_in=w_in, pool_w=pool_w, pool_scale=pool_scale, ssd_conv_w=ssd_conv_w,
             ssd_conv_b=ssd_conv_b, ssd_dt_bias=ssd_dt_bias, ssd_a_log=ssd_a_log, ssd_d=ssd_d, ssd_norm=ssd_norm,
             lru_conv_w=lru_conv_w, lru_conv_b=lru_conv_b, lru_wa=lru_wa, lru_ba=lru_ba, lru_wx=lru_wx,
             lru_bx=lru_bx, lru_lambda=lru_lambda, w_out=w_out, norm_mem=norm_mem, w_mem_q=w_mem_q,
             w_mem_k=w_mem_k, w_mem_v=w_mem_v, w_mem_o=w_mem_o, norm_ffn=norm_ffn, w_ffn_gate=w_ffn_gate,
             w_ffn_up=w_ffn_up, w_ffn_down=w_ffn_down, norm_final=norm_final)
    layers = [_layer_params(l, P) for l in range(DEPTH)]
    (y_prompt, p_pool, p_sconv, p_ssd, p_lconv, p_lru, p_mem_k, p_mem_v) = _trunk_prompt(
        x_prompt, mem_prompt, P, layers)
    (y_sample, s_pool, s_sconv, s_ssd, s_lconv, s_lru) = _trunk_sample(
        x_sample, PAST_LEN, state_pool, state_ssd_conv, state_ssd, state_lru_conv, state_lru,
        cache_mem_k, cache_mem_v, P, layers)
    return (y_prompt, y_sample, p_pool, p_sconv, p_ssd, p_lconv, p_lru, p_mem_k, p_mem_v,
            s_pool, s_sconv, s_ssd, s_lconv, s_lru)
```

```python
import functools
import itertools
import math

import numpy as np
import jax
import jax.numpy as jnp
from jax import lax
from jax.experimental import pallas as pl
from jax.experimental.pallas import tpu as pltpu

F32 = jnp.float32
BF16 = jnp.bfloat16

D_MODEL = 1024
DEPTH = 4
EPS = 1e-6
POOL_WIDTH = 512
POOL_WINDOWS = (2, 4, 8, 16)
POOL_GDIM = 128
POOL_BUF = 15
SSD_WIDTH = 1024
SSD_HEAD_DIM = 64
SSD_HEADS = 16
SSD_GROUPS = 2
SSD_GWIDTH = SSD_WIDTH // SSD_GROUPS
SSD_STATE = 128
SSD_CONV_DIM = 1536
CONV_WIDTH = 4
LRU_WIDTH = 512
LRU_HEADS = 8
LRU_HDIM = 64
LRU_C = 8.0
N_MEM = 256
MEM_HEADS = 4
MEM_HDIM = 256
D_FF = 2816
D_MIX = 2048
PAST_LEN = 16384

C_POOL = 0
C_Z = 512
C_XBC = 1536
C_GATE = 3072
C_LRU = 3584
C_DT = 4096
N_PROJ = 4224

LANES = 128
SUBLANES = 8
SUBCHUNK = 128
CONV_TILE = SUBLANES
POOL_TILE = 2 * SUBLANES
VMEM_LIMIT = 56 * 1024 * 1024


def _cparams(sem):
    return pltpu.CompilerParams(dimension_semantics=sem, vmem_limit_bytes=VMEM_LIMIT)


def _resident(shape):
    nd = len(shape)
    return pl.BlockSpec(shape, lambda *_: (0,) * nd, pipeline_mode=pl.Buffered(1))


def _layer_resident(w):
    stacked, layer = w
    nd = stacked.ndim - 1
    return pl.BlockSpec((None,) + stacked.shape[1:], lambda *_: (layer,) + (0,) * nd,
                        pipeline_mode=pl.Buffered(1))


def _rms(x, g):
    var = jnp.mean(x * x, axis=-1, keepdims=True)
    return x * lax.rsqrt(var + EPS) * g


def _silu(x):
    return x * jax.nn.sigmoid(x)


def _softplus(x):
    return jnp.maximum(x, 0.0) + jnp.log1p(jnp.exp(-jnp.abs(x)))


def _gelu_tanh(x):
    c = math.sqrt(2.0 / math.pi)
    return x * (0.5 * (1.0 + jnp.tanh(c * (x + 0.044715 * (x * x * x)))))


def _split3(x):
    x1 = x.astype(BF16)
    r1 = x - x1.astype(F32)
    x2 = r1.astype(BF16)
    x3 = (r1 - x2.astype(F32)).astype(BF16)
    return x1, x2, x3


def _sel_dot(sel, x):
    x1, x2, x3 = _split3(x)
    d = functools.partial(jnp.dot, preferred_element_type=F32)
    return d(sel, x1) + d(sel, x2) + d(sel, x3)


def _spread(x, sel2):
    hi = x.astype(BF16)
    lo = (x - hi.astype(F32)).astype(BF16)
    return jnp.dot(jnp.concatenate([hi, lo], axis=1), sel2, preferred_element_type=F32)


def _col_chunks(n, step=512):
    return [(j, min(step, n - j)) for j in range(0, n, step)]


def _linear_kernel(*refs, has_gain, has_res):
    x_ref, w_ref = refs[0], refs[1]
    k = 2
    g_ref = r_ref = None
    if has_gain:
        g_ref = refs[k]; k += 1
    if has_res:
        r_ref = refs[k]; k += 1
    o_ref = refs[k]
    x = x_ref[...]
    if has_gain:
        x = _rms(x.astype(F32), g_ref[...])
    xb = x.astype(BF16)
    for j, n in _col_chunks(o_ref.shape[1]):
        y = jnp.dot(xb, w_ref[:, j:j + n], preferred_element_type=F32)
        if has_res:
            y = y + r_ref[:, j:j + n]
        o_ref[:, j:j + n] = y.astype(o_ref.dtype)


def _linear(x, w, *, name, gain=None, residual=None, out_dtype=F32, tm=512):
    rows, kdim = x.shape
    n = w[0].shape[2]
    tm = min(tm, rows)
    assert rows % tm == 0
    in_specs = [pl.BlockSpec((tm, kdim), lambda i: (i, 0)), _layer_resident(w)]
    args = [x, w[0]]
    if gain is not None:
        in_specs.append(_resident((1, kdim)))
        args.append(gain.reshape(1, kdim))
    if residual is not None:
        in_specs.append(pl.BlockSpec((tm, n), lambda i: (i, 0)))
        args.append(residual)
    return pl.pallas_call(
        functools.partial(_linear_kernel, has_gain=gain is not None, has_res=residual is not None),
        grid=(rows // tm,),
        in_specs=in_specs,
        out_specs=pl.BlockSpec((tm, n), lambda i: (i, 0)),
        out_shape=jax.ShapeDtypeStruct((rows, n), out_dtype),
        name=name,
        compiler_params=_cparams(("parallel",)),
    )(*args)


def _kv_kernel(x_ref, wk_ref, wv_ref, k_ref, v_ref):
    xb = x_ref[...].astype(BF16)
    k_ref[...] = jnp.dot(xb, wk_ref[...], preferred_element_type=F32)
    v_ref[...] = jnp.dot(xb, wv_ref[...], preferred_element_type=F32)


def _kv_proj(mem2d, wk, wv, tm=512):
    rows = mem2d.shape[0]
    tm = min(tm, rows)
    wspec = pl.BlockSpec((None, D_MODEL, D_MODEL), lambda l, i: (l, 0, 0))
    ospec = pl.BlockSpec((None, tm, D_MODEL), lambda l, i: (l, i, 0))
    oshape = jax.ShapeDtypeStruct((DEPTH, rows, D_MODEL), F32)
    return pl.pallas_call(
        _kv_kernel,
        grid=(DEPTH, rows // tm),
        in_specs=[pl.BlockSpec((tm, D_MODEL), lambda l, i: (i, 0)), wspec, wspec],
        out_specs=[ospec, ospec],
        out_shape=[oshape, oshape],
        name="kv_proj",
        compiler_params=_cparams(("parallel", "parallel")),
    )(mem2d, wk, wv)


def _ffn_kernel(x_ref, g_ref, wg_ref, wu_ref, wd_ref, *rest):
    o_ref = rest[-1]
    x = x_ref[...]
    ub = _rms(x, g_ref[...]).astype(BF16)
    acc = x
    for j, n in _col_chunks(D_FF, 1408):
        gate = jnp.dot(ub, wg_ref[:, j:j + n], preferred_element_type=F32)
        up = jnp.dot(ub, wu_ref[:, j:j + n], preferred_element_type=F32)
        act = (_silu(gate) * up).astype(BF16)
        acc = acc + jnp.dot(act, wd_ref[j:j + n, :], preferred_element_type=F32)
    if len(rest) == 2:
        acc = _rms(acc, rest[0][...])
    o_ref[...] = acc


def _ffn(h, gain, wg, wu, wd, final_gain=None, tm=512):
    rows = h.shape[0]
    tm = min(tm, rows)
    row_spec = pl.BlockSpec((tm, D_MODEL), lambda i: (i, 0))
    in_specs = [row_spec, _resident((1, D_MODEL)), _layer_resident(wg), _layer_resident(wu), _layer_resident(wd)]
    args = [h, gain.reshape(1, D_MODEL), wg[0], wu[0], wd[0]]
    if final_gain is not None:
        in_specs.append(_resident((1, D_MODEL)))
        args.append(final_gain.reshape(1, D_MODEL))
    return pl.pallas_call(
        _ffn_kernel,
        grid=(rows // tm,),
        in_specs=in_specs,
        out_specs=row_spec,
        out_shape=jax.ShapeDtypeStruct((rows, D_MODEL), F32),
        name="ffn",
        compiler_params=_cparams(("parallel",)),
    )(*args)


def _attn_block_kernel(h_ref, g_ref, wq_ref, wo_ref, k_ref, v_ref, o_ref, kb_ref, vb_ref):
    scale = MEM_HDIM ** -0.5

    @pl.when(pl.program_id(1) == 0)
    def _cast_memory():
        kb_ref[...] = k_ref[0].astype(BF16)
        vb_ref[...] = v_ref[0].astype(BF16)

    x = h_ref[...]
    q = jnp.dot(_rms(x, g_ref[...]).astype(BF16), wq_ref[...], preferred_element_type=F32).astype(BF16)
    head_cols = [slice(hd * MEM_HDIM, (hd + 1) * MEM_HDIM) for hd in range(MEM_HEADS)]
    scores = [lax.dot_general(q[:, cols], kb_ref[:, cols], (((1,), (1,)), ((), ())),
                              preferred_element_type=F32) * scale for cols in head_cols]
    probs = []
    for sc in scores:
        e = jnp.exp(sc - jnp.max(sc, axis=-1, keepdims=True))
        probs.append((e / jnp.sum(e, axis=-1, keepdims=True)).astype(BF16))
    heads = [jnp.dot(p, vb_ref[:, cols], preferred_element_type=F32).astype(BF16)
             for p, cols in zip(probs, head_cols)]
    o = jnp.concatenate(heads, axis=1)
    o_ref[...] = x + jnp.dot(o, wo_ref[...], preferred_element_type=F32)


def _attn_block(h, gain, wq, wo, k, v, layer, *, tq, seq_rows):
    rows = h.shape[0]
    nseq = rows // seq_rows
    qblocks = seq_rows // tq
    h_spec = pl.BlockSpec((tq, D_MODEL), lambda i, c: (i * qblocks + c, 0))
    kv_spec = pl.BlockSpec((None, 1, N_MEM, D_MODEL), lambda i, c: (layer, i, 0, 0))
    return pl.pallas_call(
        _attn_block_kernel,
        grid=(nseq, qblocks),
        in_specs=[h_spec, _resident((1, D_MODEL)), _layer_resident(wq), _layer_resident(wo), kv_spec, kv_spec],
        out_specs=h_spec,
        out_shape=jax.ShapeDtypeStruct((rows, D_MODEL), F32),
        scratch_shapes=[pltpu.VMEM((N_MEM, D_MODEL), BF16), pltpu.VMEM((N_MEM, D_MODEL), BF16)],
        name="attn_block",
        compiler_params=_cparams(("parallel", "arbitrary")),
    )(h, gain.reshape(1, D_MODEL), wq[0], wo[0], k, v)


KV_SUB = 2 * MEM_HEADS
KV_ROWS = N_MEM * KV_SUB


def _cache_view(c):
    d, n = c.shape[:2]
    c = c.reshape(d, n, N_MEM, MEM_HEADS, MEM_HDIM // LANES, LANES)
    return jnp.transpose(c, (0, 1, 2, 4, 3, 5)).reshape(d, n, KV_ROWS, LANES)


def _cache_attn_kernel(q_ref, k_ref, v_ref, o_ref, *, nseq):
    scale = MEM_HDIM ** -0.5
    nvreg = KV_ROWS // LANES
    col_j = lax.broadcasted_iota(jnp.int32, (1, KV_ROWS), 1) % KV_SUB
    real = col_j < MEM_HEADS
    ts = []
    for s in range(nseq):
        q = q_ref[s * SUBLANES:(s + 1) * SUBLANES, :]
        pieces = []
        for j in range(KV_SUB):
            blk = (j % MEM_HEADS) * 2 + j // MEM_HEADS
            pieces.append(q[:, blk * LANES:(blk + 1) * LANES])
        qm = jnp.concatenate(pieces, axis=0).astype(BF16)
        kb = k_ref[s].astype(BF16)
        sc = lax.dot_general(qm, kb, (((1,), (1,)), ((), ())), preferred_element_type=F32)
        t = sc[0:SUBLANES]
        for j in range(1, KV_SUB):
            t = jnp.where(col_j == j, sc[j * SUBLANES:(j + 1) * SUBLANES], t)
        ts.append(t)
    t = jnp.concatenate(ts, axis=0)
    t = (t + pltpu.roll(t, KV_ROWS - MEM_HEADS, 1)) * scale
    t = jnp.where(real, t, jnp.finfo(F32).min)
    m = t[:, 0:LANES]
    for i in range(1, nvreg):
        m = jnp.maximum(m, t[:, i * LANES:(i + 1) * LANES])
    d = KV_SUB
    while d < LANES:
        m = jnp.maximum(m, pltpu.roll(m, d, 1))
        d *= 2
    e = jnp.where(real, jnp.exp(t - jnp.concatenate([m] * nvreg, axis=1)), 0.0)
    z = e[:, 0:LANES]
    for i in range(1, nvreg):
        z = z + e[:, i * LANES:(i + 1) * LANES]
    d = KV_SUB
    while d < LANES:
        z = z + pltpu.roll(z, d, 1)
        d *= 2
    z = jnp.where(real[:, 0:LANES], z, 1.0)
    p = e / jnp.concatenate([z] * nvreg, axis=1)
    p = p + pltpu.roll(p, MEM_HEADS, 1)
    for s in range(nseq):
        p_s = p[s * SUBLANES:(s + 1) * SUBLANES]
        pm = jnp.concatenate([jnp.where(col_j == j, p_s, 0.0) for j in range(KV_SUB)], axis=0).astype(BF16)
        o = jnp.dot(pm, v_ref[s].astype(BF16), preferred_element_type=F32)
        for blk in range(KV_SUB):
            j = (blk % 2) * MEM_HEADS + blk // 2
            o_ref[s * SUBLANES:(s + 1) * SUBLANES, blk * LANES:(blk + 1) * LANES] = (
                o[j * SUBLANES:(j + 1) * SUBLANES, :].astype(o_ref.dtype))


def _cache_attn(q, k, v, layer, *, nseq_blk=4):
    rows = q.shape[0]
    nseq = rows // SUBLANES
    assert q.shape[1] == KV_SUB * LANES and nseq % nseq_blk == 0
    q_spec = pl.BlockSpec((nseq_blk * SUBLANES, D_MODEL), lambda i: (i, 0))
    kv_spec = pl.BlockSpec((None, nseq_blk, KV_ROWS, LANES), lambda i: (layer, i, 0, 0))
    return pl.pallas_call(
        functools.partial(_cache_attn_kernel, nseq=nseq_blk),
        grid=(nseq // nseq_blk,),
        in_specs=[q_spec, kv_spec, kv_spec],
        out_specs=q_spec,
        out_shape=jax.ShapeDtypeStruct((rows, D_MODEL), BF16),
        name="cache_attn",
        compiler_params=_cparams(("parallel",)),
    )(q, k, v)


def _causal_conv(xt, carry, w_ref, b_ref, cols, *, T):
    n = xt.shape[-1]
    sub = lax.broadcasted_iota(jnp.int32, (1, SUBLANES, 1), 1)
    y = b_ref[:, cols].reshape(1, 1, n) + w_ref[CONV_WIDTH - 1:CONV_WIDTH, cols].reshape(1, 1, n) * xt
    for d in range(1, CONV_WIDTH):
        cur = pltpu.roll(xt, d, 1)
        before = pltpu.roll(carry, d, 1)
        if T != SUBLANES:
            before = jnp.concatenate([before, cur[:-1]], axis=0)
        tap = w_ref[CONV_WIDTH - 1 - d:CONV_WIDTH - d, cols].reshape(1, 1, n)
        y = y + tap * jnp.where(sub < d, before, cur)
    return y


def _conv_carry(xt, *, T, LV):
    if T != SUBLANES:
        assert LV == T
        return xt[-1:]
    assert LV >= CONV_WIDTH - 1
    return xt if LV == SUBLANES else pltpu.roll(xt, SUBLANES - LV, 1)


def _mix_chunk(x, ycat_ref, c, refs, *, S, T, NC, LV, pos0, side_work=(), late_work=()):
    pending = [iter(side_work)]

    def tick():
        task = next(pending[0], None)
        if task is not None:
            task()

    (pool_st, sconv_st, ssd_st, lconv_st, lru_st,
     pool_w, pool_scale, sconv_w, sconv_b, dt_bias, a_log, d_skip, ssd_norm,
     lconv_w, lconv_b, gate_w, gate_b, lam, expand, ssd_prev,
     out_ref, pool_out, sconv_out, ssd_out, lconv_out, lru_out,
     ext_pool, ext_sconv, ext_lconv, lru_carry, h_state) = refs
    R = S * T
    carried = NC > 1
    row = lax.broadcasted_iota(jnp.int32, (R, 1), 0)
    t_in = row % T
    pos = pos0 + c * LV + t_in

    u_pool = x[:, C_POOL:C_POOL + POOL_WIDTH]
    ext_pool[:, POOL_TILE:, :] = u_pool.reshape(S, T, POOL_WIDTH)
    for g, w in enumerate(POOL_WINDOWS):
        cols = slice(g * POOL_GDIM, (g + 1) * POOL_GDIM)
        win = ext_pool[:, :, cols].reshape(S * (POOL_TILE + T), POOL_GDIM)
        shift = 1
        while shift < w:
            win = win + pltpu.roll(win, shift, 0)
            shift *= 2
        acc = win.reshape(S, POOL_TILE + T, POOL_GDIM)[:, POOL_TILE:, :]
        cnt = jnp.minimum(pos + 1, w).astype(F32)
        pooled = acc.reshape(R, POOL_GDIM) / cnt - u_pool[:, cols]
        y = jnp.dot(pooled.astype(BF16), pool_w[g], preferred_element_type=F32)
        ycat_ref[:, cols] = (y * pool_scale[:, cols]).astype(ycat_ref.dtype)
    new_pool = ext_pool[:, pl.ds(LV, POOL_TILE), :]
    ext_pool[:, 0:POOL_TILE, :] = new_pool

    conv_blocks = []
    for j, n in _col_chunks(SSD_CONV_DIM):
        cols = slice(j, j + n)
        xt = x[:, C_XBC + j:C_XBC + j + n].reshape(R // SUBLANES, SUBLANES, n)
        conv = _causal_conv(xt, ext_sconv[:, :, cols], sconv_w, sconv_b, cols, T=T)
        ext_sconv[:, :, cols] = _conv_carry(xt, T=T, LV=LV)
        conv_blocks.append(_silu(conv).reshape(R, n))
    xbc = jnp.concatenate(conv_blocks, axis=1)

    lane = lax.broadcasted_iota(jnp.int32, (1, LANES), 1)
    dt = _softplus(x[:, C_DT:C_DT + LANES] + dt_bias[...])
    dt = jnp.where((lane < SSD_HEADS) & (t_in < LV), dt, 0.0)
    d_a = dt * (-jnp.exp(a_log[...]))
    z = x[:, C_Z:C_Z + SSD_WIDTH]

    li = lax.broadcasted_iota(jnp.int32, (SUBCHUNK, SUBCHUNK), 0)
    si = lax.broadcasted_iota(jnp.int32, (SUBCHUNK, SUBCHUNK), 1)
    same_seq = (li // T) == (si // T) if S > 1 else (li >= 0)
    causal = same_seq & (si <= li)
    causal_b = causal.astype(BF16)
    same_b = same_seq.astype(BF16)
    lane_q = lax.broadcasted_iota(jnp.int32, (SUBCHUNK, LANES), 1)
    row_q = lax.broadcasted_iota(jnp.int32, (SUBCHUNK, 1), 0)
    expand_m = expand[...]

    for q in range(R // SUBCHUNK):
        rs = slice(q * SUBCHUNK, (q + 1) * SUBCHUNK)
        xs = xbc[rs, 0:SSD_WIDTH]
        d_a_q = d_a[rs]
        acs = _sel_dot(causal_b, d_a_q)
        tot = _sel_dot(same_b, d_a_q)
        acs_row = acs.T
        dt_e = _spread(dt[rs], expand_m)
        eacs_e = _spread(jnp.exp(acs), expand_m)
        etot = jnp.exp(tot)
        etot_e = _spread(etot[0:SUBLANES] if carried else etot, expand_m)
        xdt = xs * dt_e
        xdt_b = xdt.astype(BF16)
        xdecay = xdt * _spread(jnp.exp(tot - acs), expand_m)
        xdecay_b = xdecay.astype(BF16)
        tick()
        y_parts = []
        for g in range(SSD_GROUPS):
            gc = slice(g * SSD_GWIDTH, (g + 1) * SSD_GWIDTH)
            b_g = xbc[rs, SSD_WIDTH + g * SSD_STATE:SSD_WIDTH + (g + 1) * SSD_STATE].astype(BF16)
            c_g = xbc[rs, SSD_WIDTH + (SSD_GROUPS + g) * SSD_STATE:
                      SSD_WIDTH + (SSD_GROUPS + g + 1) * SSD_STATE].astype(BF16)
            cb = lax.dot_general(c_g, b_g, (((1,), (1,)), ((), ())), preferred_element_type=F32)
            diag = []
            for j in range(SSD_GWIDTH // LANES):
                pair = []
                for hh in range(2):
                    k = g * (SSD_HEADS // SSD_GROUPS) + 2 * j + hh
                    seg = acs[:, k:k + 1] - acs_row[k:k + 1, :]
                    lmat = jnp.where(causal, jnp.exp(jnp.where(causal, seg, 0.0)), 0.0)
                    pair.append((cb * lmat).astype(BF16))
                xp = xdt_b[:, g * SSD_GWIDTH + j * LANES:g * SSD_GWIDTH + (j + 1) * LANES]
                top = jnp.where(lane_q < SSD_HEAD_DIM, xp, jnp.zeros_like(xp))
                bot = jnp.where(lane_q >= SSD_HEAD_DIM, xp, jnp.zeros_like(xp))
                diag.append(jnp.dot(jnp.concatenate(pair, axis=1), jnp.concatenate([top, bot], axis=0),
                                    preferred_element_type=F32))
            y_g = jnp.concatenate(diag, axis=1)
            if carried:
                h_t = h_state[0, g]
                y_g = y_g + jnp.dot(c_g, h_t.astype(BF16), preferred_element_type=F32) * eacs_e[:, gc]
                upd = lax.dot_general(b_g, xdecay_b[:, gc], (((0,), (0,)), ((), ())),
                                      preferred_element_type=F32)
                h_state[0, g] = h_t * etot_e[0:1, gc] + upd
            else:
                xd_t = xdecay[:, gc].T.astype(BF16)
                etot_t = etot_e[:, gc].T
                y_off = jnp.zeros((SUBCHUNK, SSD_GWIDTH), F32)
                for s in range(S):
                    mine = (row_q // T) == s
                    h_s = ssd_st[s, gc, :]
                    c_s = jnp.where(mine, c_g, jnp.zeros_like(c_g))
                    b_s = jnp.where(mine, b_g, jnp.zeros_like(b_g))
                    y_off = y_off + lax.dot_general(c_s, h_s.astype(BF16), (((1,), (1,)), ((), ())),
                                                    preferred_element_type=F32)
                    upd = jnp.dot(xd_t, b_s, preferred_element_type=F32)
                    ssd_out[s, gc, :] = h_s * etot_t[:, s * T:s * T + 1] + upd
                y_g = y_g + y_off * eacs_e[:, gc]
            y_parts.append(y_g)
            tick()
        y = jnp.concatenate(y_parts, axis=1) + xs * d_skip[...]
        y = _rms(y * _silu(z[rs]), ssd_norm[...])
        ycat_ref[rs, POOL_WIDTH:POOL_WIDTH + SSD_WIDTH] = y.astype(ycat_ref.dtype)

    pending[0] = itertools.chain(pending[0], late_work)
    xt = x[:, C_LRU:C_LRU + LRU_WIDTH].reshape(R // SUBLANES, SUBLANES, LRU_WIDTH)
    xc = _causal_conv(xt, ext_lconv[...], lconv_w, lconv_b, slice(0, LRU_WIDTH), T=T).reshape(R, LRU_WIDTH)
    ext_lconv[...] = _conv_carry(xt, T=T, LV=LV)
    tick()
    gates = jnp.dot(xc.astype(BF16), gate_w[...], preferred_element_type=F32) + gate_b[...]
    r_gate = jax.nn.sigmoid(gates[:, 0:LRU_WIDTH])
    i_gate = jax.nn.sigmoid(gates[:, LRU_WIDTH:2 * LRU_WIDTH])
    log_a = (-LRU_C) * r_gate * _softplus(-lam[...])
    a = jnp.exp(log_a)
    mult = jnp.where(pos == 0, 1.0, jnp.sqrt(1.0 - jnp.exp(2.0 * log_a)))
    b = mult * i_gate * xc
    tick()
    tiles = R // SUBLANES
    a3 = a.reshape(tiles, SUBLANES, LRU_WIDTH)
    b3 = b.reshape(tiles, SUBLANES, LRU_WIDTH)
    sub = lax.broadcasted_iota(jnp.int32, (1, SUBLANES, 1), 1)
    d = 1
    while d < SUBLANES:
        keep = sub >= d
        a_prev = jnp.where(keep, pltpu.roll(a3, d, 1), 1.0)
        b_prev = jnp.where(keep, pltpu.roll(b3, d, 1), 0.0)
        b3 = a3 * b_prev + b3
        a3 = a3 * a_prev
        d *= 2
        tick()
    if S == 1:
        h_prev = lru_carry[0, 0:1, :]
        h_tiles = []
        for i in range(tiles):
            h_i = a3[i] * h_prev + b3[i]
            h_tiles.append(h_i)
            h_prev = h_i[SUBLANES - 1:SUBLANES, :]
        h = jnp.concatenate(h_tiles, axis=0)
    else:
        h = (a3 * lru_carry[...] + b3).reshape(R, LRU_WIDTH)
    gate = x[:, C_GATE:C_GATE + LRU_WIDTH]
    ycat_ref[:, POOL_WIDTH + SSD_WIDTH:D_MIX] = (h * _gelu_tanh(gate)).astype(ycat_ref.dtype)
    if S == 1:
        lru_carry[0] = jnp.broadcast_to(h[LV - 1:LV, :], (SUBLANES, LRU_WIDTH))
    else:
        li_r = lax.broadcasted_iota(jnp.int32, (R, R), 0)
        si_r = lax.broadcasted_iota(jnp.int32, (R, R), 1)
        pick_last = ((li_r // T) == (si_r // T)) & ((si_r % T) == LV - 1)
        lru_carry[...] = _sel_dot(pick_last.astype(BF16), h).reshape(S, T, LRU_WIDTH)
    for task in pending[0]:
        task()


def _mix_kernel(*refs, S, T, NC, LV, pos0, fused):
    refs = list(refs)
    if fused:
        h_ref, h_next, norm_g, w_in, w_out = refs[:5]
        del refs[:5]
    else:
        x = refs.pop(0)
    io = tuple(refs[:31])
    (pool_st, sconv_st, ssd_st, lconv_st, lru_st) = io[:5]
    (out_ref, pool_out, sconv_out, ssd_out, lconv_out, lru_out,
     ext_pool, ext_sconv, ext_lconv, lru_carry, h_state) = io[20:]
    cfg = dict(S=S, T=T, NC=NC, LV=LV, pos0=pos0)
    c = pl.program_id(1)
    carried = NC > 1

    @pl.when(c == 0)
    def _load_state():
        ext_pool[:, 0:POOL_TILE, :] = pool_st[...]
        ext_sconv[:, 0:CONV_TILE, :] = sconv_st[...]
        ext_lconv[:, 0:CONV_TILE, :] = lconv_st[...]
        lru_carry[...] = lru_st[...]
        if carried:
            for g in range(SSD_GROUPS):
                h_state[0, g] = ssd_st[0, g * SSD_GWIDTH:(g + 1) * SSD_GWIDTH, :].T

    if not fused:
        _mix_chunk(x, out_ref, c, io, **cfg)
    else:
        proj_even, proj_odd, ycat_ref = refs[31:34]

        def projection_steps(src, dst):
            normed = []

            def column_step(j, n):
                if not normed:
                    normed.append(_rms(src[...], norm_g[...]).astype(BF16))
                dst[:, j:j + n] = jnp.dot(normed[0], w_in[:, j:j + n], preferred_element_type=F32)

            return [functools.partial(column_step, j, n) for j, n in _col_chunks(N_PROJ)]

        @pl.when(c == 0)
        def _first_projection():
            for step in projection_steps(h_ref, proj_even):
                step()

        for parity, cur, nxt in ((0, proj_even, proj_odd), (1, proj_odd, proj_even)):
            @pl.when(c % 2 == parity)
            def _chunk(cur=cur, nxt=nxt):
                early = POOL_WIDTH + SSD_WIDTH

                def project_early_columns():
                    out_ref[...] = h_ref[...] + jnp.dot(ycat_ref[:, 0:early], w_out[0:early, :],
                                                        preferred_element_type=F32)

                _mix_chunk(cur, ycat_ref, c, io, side_work=projection_steps(h_next, nxt),
                           late_work=[project_early_columns], **cfg)
                out_ref[...] += jnp.dot(ycat_ref[:, early:D_MIX], w_out[early:D_MIX, :],
                                        preferred_element_type=F32)

    @pl.when(c == NC - 1)
    def _store_state():
        pool_out[...] = ext_pool[:, 0:POOL_TILE, :]
        sconv_out[...] = ext_sconv[:, 0:CONV_TILE, :]
        lconv_out[...] = ext_lconv[:, 0:CONV_TILE, :]
        lru_out[...] = lru_carry[...]
        if carried:
            for g in range(SSD_GROUPS):
                ssd_out[0, g * SSD_GWIDTH:(g + 1) * SSD_GWIDTH, :] = h_state[0, g].T


def _mix(x, st, ssd_state, ssd_layer, ssd_new, out_layer, w, *, S, T, NC, LV, pos0, proj=None):
    rows = x.shape[0]
    nseq = rows // (NC * T)
    R = S * T
    fused = proj is not None
    assert nseq % S == 0 and R % SUBCHUNK == 0
    assert (S == 1) or (NC == 1 and R == SUBCHUNK and T == SUBLANES)
    seq3 = lambda i, c: (i, 0, 0)
    row_blk = lambda i, c: (i * NC + c, 0)
    if fused:
        next_blk = lambda i, c: (i * NC + jnp.minimum(c + 1, NC - 1), 0)
        lead_specs = [pl.BlockSpec((R, D_MODEL), row_blk), pl.BlockSpec((R, D_MODEL), next_blk),
                      _resident((1, D_MODEL)), _layer_resident(proj[1]), _layer_resident(proj[2])]
        lead_args = [x, x, proj[0].reshape(1, D_MODEL), proj[1][0], proj[2][0]]
    else:
        lead_specs = [pl.BlockSpec((R, N_PROJ), row_blk)]
        lead_args = [x]
    in_specs = lead_specs + [
        pl.BlockSpec((S, POOL_TILE, POOL_WIDTH), seq3),
        pl.BlockSpec((S, CONV_TILE, SSD_CONV_DIM), seq3),
        pl.BlockSpec((None, S, SSD_WIDTH, SSD_STATE), lambda i, c: (ssd_layer, i, 0, 0)),
        pl.BlockSpec((S, CONV_TILE, LRU_WIDTH), seq3),
        pl.BlockSpec((S, SUBLANES, LRU_WIDTH), seq3),
    ] + [_resident(a.shape) for a in w] + [pl.BlockSpec(memory_space=pl.ANY)]
    if ssd_new is None:
        ssd_new = jnp.zeros((DEPTH, nseq, SSD_WIDTH, SSD_STATE), F32)
    aliases = {len(in_specs) - 1: 3}
    out_specs = [
        pl.BlockSpec((R, D_MODEL if fused else D_MIX), row_blk),
        pl.BlockSpec((S, POOL_TILE, POOL_WIDTH), seq3),
        pl.BlockSpec((S, CONV_TILE, SSD_CONV_DIM), seq3),
        pl.BlockSpec((None, S, SSD_WIDTH, SSD_STATE), lambda i, c: (out_layer, i, 0, 0)),
        pl.BlockSpec((S, CONV_TILE, LRU_WIDTH), seq3),
        pl.BlockSpec((S, SUBLANES, LRU_WIDTH), seq3),
    ]
    out_shape = [
        jax.ShapeDtypeStruct((rows, D_MODEL), F32) if fused else jax.ShapeDtypeStruct((rows, D_MIX), BF16),
        jax.ShapeDtypeStruct((nseq, POOL_TILE, POOL_WIDTH), F32),
        jax.ShapeDtypeStruct((nseq, CONV_TILE, SSD_CONV_DIM), F32),
        jax.ShapeDtypeStruct((DEPTH, nseq, SSD_WIDTH, SSD_STATE), F32),
        jax.ShapeDtypeStruct((nseq, CONV_TILE, LRU_WIDTH), F32),
        jax.ShapeDtypeStruct((nseq, SUBLANES, LRU_WIDTH), F32),
    ]
    h_scratch = (1, SSD_GROUPS, SSD_STATE, SSD_GWIDTH) if NC > 1 else (1, 1, SUBLANES, LANES)
    scratch = [
        pltpu.VMEM((S, POOL_TILE + T, POOL_WIDTH), F32),
        pltpu.VMEM((S, CONV_TILE, SSD_CONV_DIM), F32),
        pltpu.VMEM((S, CONV_TILE, LRU_WIDTH), F32),
        pltpu.VMEM((S, SUBLANES, LRU_WIDTH), F32),
        pltpu.VMEM(h_scratch, F32),
    ]
    if fused:
        scratch += [pltpu.VMEM((R, N_PROJ), F32), pltpu.VMEM((R, N_PROJ), F32), pltpu.VMEM((R, D_MIX), BF16)]
    return pl.pallas_call(
        functools.partial(_mix_kernel, S=S, T=T, NC=NC, LV=LV, pos0=pos0, fused=fused),
        grid=(nseq // S, NC),
        in_specs=in_specs,
        out_specs=out_specs,
        out_shape=out_shape,
        scratch_shapes=scratch,
        input_output_aliases=aliases,
        name="mix_seq%d" % S,
        compiler_params=_cparams(("parallel", "arbitrary")),
    )(*lead_args, st[0], st[1], ssd_state, st[2], st[3], *w, ssd_new)


def _expand_matrix():
    e = np.zeros((LANES, SSD_WIDTH), np.float32)
    for k in range(SSD_HEADS):
        e[k, k * SSD_HEAD_DIM:(k + 1) * SSD_HEAD_DIM] = 1.0
    return jnp.asarray(np.concatenate([e, e], axis=0), BF16)


def _block_diag(w):
    eye = jnp.eye(LRU_HEADS, dtype=w.dtype)
    return jnp.einsum('hij,hg->higj', w, eye).reshape(LRU_WIDTH, LRU_WIDTH)


def _pad_lanes(v):
    return jnp.pad(v, (0, LANES - v.shape[0])).reshape(1, LANES)


def _matmul_weights(P):
    w_in = P['w_in']
    off_dt = 3072
    w_in = jnp.concatenate(
        [w_in[:, :, :off_dt], w_in[:, :, off_dt + SSD_HEADS:], w_in[:, :, off_dt:off_dt + SSD_HEADS],
         jnp.zeros((DEPTH, D_MODEL, N_PROJ - C_DT - SSD_HEADS), F32)], axis=2)
    named = dict(w_in=w_in, w_out=P['w_out'], wq=P['w_mem_q'], wo=P['w_mem_o'], wg=P['w_ffn_gate'],
                 wu=P['w_ffn_up'], wd=P['w_ffn_down'])
    return {k: v.astype(BF16) for k, v in named.items()}


def _layer_params(l, P, big):
    mix_w = (
        P['pool_w'][l].astype(BF16),
        P['pool_scale'][l].reshape(1, POOL_WIDTH),
        P['ssd_conv_w'][l],
        P['ssd_conv_b'][l].reshape(1, SSD_CONV_DIM),
        _pad_lanes(P['ssd_dt_bias'][l]),
        _pad_lanes(P['ssd_a_log'][l]),
        jnp.repeat(P['ssd_d'][l], SSD_HEAD_DIM).reshape(1, SSD_WIDTH),
        P['ssd_norm'][l].reshape(1, SSD_WIDTH),
        P['lru_conv_w'][l],
        P['lru_conv_b'][l].reshape(1, LRU_WIDTH),
        jnp.concatenate([_block_diag(P['lru_wa'][l]), _block_diag(P['lru_wx'][l])], axis=1).astype(BF16),
        jnp.concatenate([P['lru_ba'][l].reshape(1, LRU_WIDTH), P['lru_bx'][l].reshape(1, LRU_WIDTH)], axis=1),
        P['lru_lambda'][l].reshape(1, LRU_WIDTH),
        _expand_matrix(),
    )
    out = dict(mix=mix_w, norm_mix=P['norm_mix'][l], norm_mem=P['norm_mem'][l], norm_ffn=P['norm_ffn'][l])
    out.update({k: (v, l) for k, v in big.items()})
    return out


def _run_trunk(h, layers, norm_final, states, ssd_state, attend, *, mix_cfg, fuse_proj):
    new_states = []
    ssd_new = None
    for l, lp in enumerate(layers):
        ssd_layer = min(l, ssd_state.shape[0] - 1)
        if fuse_proj:
            h, *st = _mix(h, states[l], ssd_state, ssd_layer, ssd_new, l, lp['mix'],
                          proj=(lp['norm_mix'], lp['w_in'], lp['w_out']), **mix_cfg)
        else:
            proj = _linear(h, lp['w_in'], name="in_proj", gain=lp['norm_mix'])
            ycat, *st = _mix(proj, states[l], ssd_state, ssd_layer, ssd_new, l, lp['mix'], **mix_cfg)
            h = _linear(ycat, lp['w_out'], name="out_proj", residual=h)
        ssd_new = st[2]
        h = attend(h, lp, l)
        last = l == len(layers) - 1
        h = _ffn(h, lp['norm_ffn'], lp['wg'], lp['wu'], lp['wd'], final_gain=norm_final if last else None)
        new_states.append(st)
    return h, new_states


def _unpack_states(new_states, nseq):
    pool = jnp.stack([s[0][:, 1:POOL_TILE] for s in new_states])
    sconv = jnp.stack([s[1][:, CONV_TILE - 3:CONV_TILE] for s in new_states])
    ssd = new_states[-1][2].reshape(DEPTH, nseq, SSD_HEADS, SSD_HEAD_DIM, SSD_STATE)
    lconv = jnp.stack([s[3][:, CONV_TILE - 3:CONV_TILE] for s in new_states])
    lru = jnp.stack([s[4][:, 0] for s in new_states])
    return pool, sconv, ssd, lconv, lru


def _trunk_prompt(x_prompt, mem_prompt, P, layers, *, chunk=256, tq=512):
    batch, seq, _ = x_prompt.shape
    chunk = min(chunk, seq)
    tq = min(tq, seq)
    wk = P['w_mem_k'].astype(BF16)
    wv = P['w_mem_v'].astype(BF16)
    mem_k, mem_v = _kv_proj(mem_prompt.reshape(batch * N_MEM, D_MODEL), wk, wv)
    mem_k = mem_k.reshape(DEPTH, batch, N_MEM, D_MODEL)
    mem_v = mem_v.reshape(DEPTH, batch, N_MEM, D_MODEL)
    zeros = (jnp.zeros((batch, POOL_TILE, POOL_WIDTH), F32), jnp.zeros((batch, CONV_TILE, SSD_CONV_DIM), F32),
             jnp.zeros((batch, CONV_TILE, LRU_WIDTH), F32), jnp.zeros((batch, SUBLANES, LRU_WIDTH), F32))
    ssd0 = jnp.zeros((1, batch, SSD_WIDTH, SSD_STATE), F32)
    y, new_states = _run_trunk(
        x_prompt.reshape(batch * seq, D_MODEL), layers, P['norm_final'], [zeros] * DEPTH, ssd0,
        lambda h, lp, l: _attn_block(h, lp['norm_mem'], lp['wq'], lp['wo'], mem_k, mem_v, l, tq=tq, seq_rows=seq),
        mix_cfg=dict(S=1, T=chunk, NC=seq // chunk, LV=chunk, pos0=0), fuse_proj=True)
    return (y.reshape(batch, seq, D_MODEL),) + _unpack_states(new_states, batch) + (
        mem_k.reshape(DEPTH, batch, N_MEM, MEM_HEADS, MEM_HDIM),
        mem_v.reshape(DEPTH, batch, N_MEM, MEM_HEADS, MEM_HDIM))


def _trunk_sample(x_sample, past_len, state_pool, state_ssd_conv, state_ssd, state_lru_conv, state_lru,
                  cache_k, cache_v, P, layers, *, seq_blk=16, attn_blk=8):
    nseq, seq, _ = x_sample.shape
    assert seq <= SUBLANES
    pad_t = SUBLANES - seq
    h = jnp.pad(x_sample, ((0, 0), (0, pad_t), (0, 0))).reshape(nseq * SUBLANES, D_MODEL)
    states = []
    for l in range(DEPTH):
        states.append((
            jnp.pad(state_pool[l], ((0, 0), (POOL_TILE - POOL_BUF, 0), (0, 0))),
            jnp.pad(state_ssd_conv[l], ((0, 0), (CONV_TILE - 3, 0), (0, 0))),
            jnp.pad(state_lru_conv[l], ((0, 0), (CONV_TILE - 3, 0), (0, 0))),
            jnp.broadcast_to(state_lru[l][:, None, :], (nseq, SUBLANES, LRU_WIDTH)),
        ))
    kc, vc = _cache_view(cache_k), _cache_view(cache_v)

    def attend(h, lp, l):
        q = _linear(h, lp['wq'], name="q_proj", gain=lp['norm_mem'])
        return _linear(_cache_attn(q, kc, vc, l, nseq_blk=attn_blk), lp['wo'], name="o_proj", residual=h)

    y, new_states = _run_trunk(
        h, layers, P['norm_final'], states, state_ssd.reshape(DEPTH, nseq, SSD_WIDTH, SSD_STATE), attend,
        mix_cfg=dict(S=seq_blk, T=SUBLANES, NC=1, LV=seq, pos0=past_len), fuse_proj=False)
    y = y.reshape(nseq, SUBLANES, D_MODEL)[:, :seq]
    return (y,) + _unpack_states(new_states, nseq)


def kernel(x_prompt, x_sample, mem_prompt, state_pool, state_ssd_conv, state_ssd, state_lru_conv, state_lru,
           cache_mem_k, cache_mem_v, norm_mix, w_in, pool_w, pool_scale, ssd_conv_w, ssd_conv_b, ssd_dt_bias,
           ssd_a_log, ssd_d, ssd_norm, lru_conv_w, lru_conv_b, lru_wa, lru_ba, lru_wx, lru_bx, lru_lambda,
           w_out, norm_mem, w_mem_q, w_mem_k, w_mem_v, w_mem_o, norm_ffn, w_ffn_gate, w_ffn_up, w_ffn_down,
           norm_final):
    P = dict(norm_mix=norm_mix, w_in=w_in, pool_w=pool_w, pool_scale=pool_scale, ssd_conv_w=ssd_conv_w,
             ssd_conv_b=ssd_conv_b, ssd_dt_bias=ssd_dt_bias, ssd_a_log=ssd_a_log, ssd_d=ssd_d, ssd_norm=ssd_norm,
             lru_conv_w=lru_conv_w, lru_conv_b=lru_conv_b, lru_wa=lru_wa, lru_ba=lru_ba, lru_wx=lru_wx,
             lru_bx=lru_bx, lru_lambda=lru_lambda, w_out=w_out, norm_mem=norm_mem, w_mem_q=w_mem_q,
             w_mem_k=w_mem_k, w_mem_v=w_mem_v, w_mem_o=w_mem_o, norm_ffn=norm_ffn, w_ffn_gate=w_ffn_gate,
             w_ffn_up=w_ffn_up, w_ffn_down=w_ffn_down, norm_final=norm_final)
    big = _matmul_weights(P)
    layers = [_layer_params(l, P, big) for l in range(DEPTH)]
    (y_prompt, p_pool, p_sconv, p_ssd, p_lconv, p_lru, p_mem_k, p_mem_v) = _trunk_prompt(
        x_prompt, mem_prompt, P, layers)
    (y_sample, s_pool, s_sconv, s_ssd, s_lconv, s_lru) = _trunk_sample(
        x_sample, PAST_LEN, state_pool, state_ssd_conv, state_ssd, state_lru_conv, state_lru,
        cache_mem_k, cache_mem_v, P, layers)
    return (y_prompt, y_sample, p_pool, p_sconv, p_ssd, p_lconv, p_lru, p_mem_k, p_mem_v,
            s_pool, s_sconv, s_ssd, s_lconv, s_lru)
```

```python
import functools
import itertools
import math

import numpy as np
import jax
import jax.numpy as jnp
from jax import lax
from jax.experimental import pallas as pl
from jax.experimental.pallas import tpu as pltpu

F32 = jnp.float32
BF16 = jnp.bfloat16

D_MODEL = 1024
DEPTH = 4
EPS = 1e-6
POOL_WIDTH = 512
POOL_WINDOWS = (2, 4, 8, 16)
POOL_GDIM = 128
POOL_BUF = 15
SSD_WIDTH = 1024
SSD_HEAD_DIM = 64
SSD_HEADS = 16
SSD_GROUPS = 2
SSD_GWIDTH = SSD_WIDTH // SSD_GROUPS
SSD_STATE = 128
SSD_CONV_DIM = 1536
CONV_WIDTH = 4
LRU_WIDTH = 512
LRU_HEADS = 8
LRU_HDIM = 64
LRU_C = 8.0
N_MEM = 256
MEM_HEADS = 4
MEM_HDIM = 256
D_FF = 2816
D_MIX = 2048
PAST_LEN = 16384

C_POOL = 0
C_Z = 512
C_XBC = 1536
C_GATE = 3072
C_LRU = 3584
C_DT = 4096
N_PROJ = 4224

LANES = 128
SUBLANES = 8
SUBCHUNK = 128
CONV_TILE = SUBLANES
POOL_TILE = 2 * SUBLANES
VMEM_LIMIT = 56 * 1024 * 1024


def _cparams(sem):
    return pltpu.CompilerParams(dimension_semantics=sem, vmem_limit_bytes=VMEM_LIMIT)


def _resident(shape):
    nd = len(shape)
    return pl.BlockSpec(shape, lambda *_: (0,) * nd, pipeline_mode=pl.Buffered(1))


def _layer_resident(w):
    stacked, layer = w
    nd = stacked.ndim - 1
    return pl.BlockSpec((None,) + stacked.shape[1:], lambda *_: (layer,) + (0,) * nd,
                        pipeline_mode=pl.Buffered(1))


def _rms(x, g):
    var = jnp.mean(x * x, axis=-1, keepdims=True)
    return x * lax.rsqrt(var + EPS) * g


def _silu(x):
    return x * jax.nn.sigmoid(x)


def _softplus(x):
    return jnp.maximum(x, 0.0) + jnp.log1p(jnp.exp(-jnp.abs(x)))


def _gelu_tanh(x):
    c = math.sqrt(2.0 / math.pi)
    return x * (0.5 * (1.0 + jnp.tanh(c * (x + 0.044715 * (x * x * x)))))


def _split3(x):
    x1 = x.astype(BF16)
    r1 = x - x1.astype(F32)
    x2 = r1.astype(BF16)
    x3 = (r1 - x2.astype(F32)).astype(BF16)
    return x1, x2, x3


def _sel_dot(sel, x):
    x1, x2, x3 = _split3(x)
    d = functools.partial(jnp.dot, preferred_element_type=F32)
    return d(sel, x1) + d(sel, x2) + d(sel, x3)


def _spread(x, sel2):
    hi = x.astype(BF16)
    lo = (x - hi.astype(F32)).astype(BF16)
    return jnp.dot(jnp.concatenate([hi, lo], axis=1), sel2, preferred_element_type=F32)


def _col_chunks(n, step=512):
    return [(j, min(step, n - j)) for j in range(0, n, step)]


def _linear_kernel(*refs, has_gain, has_res):
    x_ref, w_ref = refs[0], refs[1]
    k = 2
    g_ref = r_ref = None
    if has_gain:
        g_ref = refs[k]; k += 1
    if has_res:
        r_ref = refs[k]; k += 1
    o_ref = refs[k]
    x = x_ref[...]
    if has_gain:
        x = _rms(x.astype(F32), g_ref[...])
    xb = x.astype(BF16)
    for j, n in _col_chunks(o_ref.shape[1]):
        y = jnp.dot(xb, w_ref[:, j:j + n], preferred_element_type=F32)
        if has_res:
            y = y + r_ref[:, j:j + n]
        o_ref[:, j:j + n] = y.astype(o_ref.dtype)


def _linear(x, w, *, name, gain=None, residual=None, out_dtype=F32, tm=512):
    rows, kdim = x.shape
    n = w[0].shape[2]
    tm = min(tm, rows)
    assert rows % tm == 0
    in_specs = [pl.BlockSpec((tm, kdim), lambda i: (i, 0)), _layer_resident(w)]
    args = [x, w[0]]
    if gain is not None:
        in_specs.append(_resident((1, kdim)))
        args.append(gain.reshape(1, kdim))
    if residual is not None:
        in_specs.append(pl.BlockSpec((tm, n), lambda i: (i, 0)))
        args.append(residual)
    return pl.pallas_call(
        functools.partial(_linear_kernel, has_gain=gain is not None, has_res=residual is not None),
        grid=(rows // tm,),
        in_specs=in_specs,
        out_specs=pl.BlockSpec((tm, n), lambda i: (i, 0)),
        out_shape=jax.ShapeDtypeStruct((rows, n), out_dtype),
        name=name,
        compiler_params=_cparams(("parallel",)),
    )(*args)


def _kv_kernel(x_ref, wk_ref, wv_ref, k_ref, v_ref):
    xb = x_ref[...].astype(BF16)
    k_ref[...] = jnp.dot(xb, wk_ref[...], preferred_element_type=F32)
    v_ref[...] = jnp.dot(xb, wv_ref[...], preferred_element_type=F32)


def _kv_proj(mem2d, wk, wv, tm=512):
    rows = mem2d.shape[0]
    tm = min(tm, rows)
    wspec = pl.BlockSpec((None, D_MODEL, D_MODEL), lambda l, i: (l, 0, 0))
    ospec = pl.BlockSpec((None, tm, D_MODEL), lambda l, i: (l, i, 0))
    oshape = jax.ShapeDtypeStruct((DEPTH, rows, D_MODEL), F32)
    return pl.pallas_call(
        _kv_kernel,
        grid=(DEPTH, rows // tm),
        in_specs=[pl.BlockSpec((tm, D_MODEL), lambda l, i: (i, 0)), wspec, wspec],
        out_specs=[ospec, ospec],
        out_shape=[oshape, oshape],
        name="kv_proj",
        compiler_params=_cparams(("parallel", "parallel")),
    )(mem2d, wk, wv)


def _ffn_kernel(x_ref, g_ref, wg_ref, wu_ref, wd_ref, *rest):
    o_ref = rest[-1]
    x = x_ref[...]
    ub = _rms(x, g_ref[...]).astype(BF16)
    acc = x
    for j, n in _col_chunks(D_FF, 1024):
        gate = jnp.dot(ub, wg_ref[:, j:j + n], preferred_element_type=F32)
        up = jnp.dot(ub, wu_ref[:, j:j + n], preferred_element_type=F32)
        act = (_silu(gate) * up).astype(BF16)
        acc = acc + jnp.dot(act, wd_ref[j:j + n, :], preferred_element_type=F32)
    if len(rest) == 2:
        acc = _rms(acc, rest[0][...])
    o_ref[...] = acc


def _ffn(h, gain, wg, wu, wd, final_gain=None, tm=1024):
    rows = h.shape[0]
    tm = min(tm, rows)
    row_spec = pl.BlockSpec((tm, D_MODEL), lambda i: (i, 0))
    in_specs = [row_spec, _resident((1, D_MODEL)), _layer_resident(wg), _layer_resident(wu), _layer_resident(wd)]
    args = [h, gain.reshape(1, D_MODEL), wg[0], wu[0], wd[0]]
    if final_gain is not None:
        in_specs.append(_resident((1, D_MODEL)))
        args.append(final_gain.reshape(1, D_MODEL))
    return pl.pallas_call(
        _ffn_kernel,
        grid=(rows // tm,),
        in_specs=in_specs,
        out_specs=row_spec,
        out_shape=jax.ShapeDtypeStruct((rows, D_MODEL), F32),
        name="ffn",
        compiler_params=_cparams(("parallel",)),
    )(*args)


def _attn_block_kernel(h_ref, g_ref, wq_ref, wo_ref, k_ref, v_ref, o_ref, kb_ref, vb_ref):
    scale = MEM_HDIM ** -0.5

    @pl.when(pl.program_id(1) == 0)
    def _cast_memory():
        kb_ref[...] = k_ref[0].astype(BF16)
        vb_ref[...] = v_ref[0].astype(BF16)

    x = h_ref[...]
    q = jnp.dot(_rms(x, g_ref[...]).astype(BF16), wq_ref[...], preferred_element_type=F32).astype(BF16)
    head_cols = [slice(hd * MEM_HDIM, (hd + 1) * MEM_HDIM) for hd in range(MEM_HEADS)]
    scores = [lax.dot_general(q[:, cols], kb_ref[:, cols], (((1,), (1,)), ((), ())),
                              preferred_element_type=F32) * scale for cols in head_cols]
    probs = []
    for sc in scores:
        e = jnp.exp(sc - jnp.max(sc, axis=-1, keepdims=True))
        probs.append((e / jnp.sum(e, axis=-1, keepdims=True)).astype(BF16))
    heads = [jnp.dot(p, vb_ref[:, cols], preferred_element_type=F32).astype(BF16)
             for p, cols in zip(probs, head_cols)]
    o = jnp.concatenate(heads, axis=1)
    o_ref[...] = x + jnp.dot(o, wo_ref[...], preferred_element_type=F32)


def _attn_block(h, gain, wq, wo, k, v, layer, *, tq, seq_rows):
    rows = h.shape[0]
    nseq = rows // seq_rows
    qblocks = seq_rows // tq
    h_spec = pl.BlockSpec((tq, D_MODEL), lambda i, c: (i * qblocks + c, 0))
    kv_spec = pl.BlockSpec((None, 1, N_MEM, D_MODEL), lambda i, c: (layer, i, 0, 0))
    return pl.pallas_call(
        _attn_block_kernel,
        grid=(nseq, qblocks),
        in_specs=[h_spec, _resident((1, D_MODEL)), _layer_resident(wq), _layer_resident(wo), kv_spec, kv_spec],
        out_specs=h_spec,
        out_shape=jax.ShapeDtypeStruct((rows, D_MODEL), F32),
        scratch_shapes=[pltpu.VMEM((N_MEM, D_MODEL), BF16), pltpu.VMEM((N_MEM, D_MODEL), BF16)],
        name="attn_block",
        compiler_params=_cparams(("parallel", "arbitrary")),
    )(h, gain.reshape(1, D_MODEL), wq[0], wo[0], k, v)


KV_SUB = 2 * MEM_HEADS
KV_ROWS = N_MEM * KV_SUB


def _cache_view(c):
    d, n = c.shape[:2]
    c = c.reshape(d, n, N_MEM, MEM_HEADS, MEM_HDIM // LANES, LANES)
    return jnp.transpose(c, (0, 1, 2, 4, 3, 5)).reshape(d, n, KV_ROWS, LANES)


def _cache_attn_kernel(q_ref, k_ref, v_ref, o_ref, *, nseq):
    scale = MEM_HDIM ** -0.5
    nvreg = KV_ROWS // LANES
    col_j = lax.broadcasted_iota(jnp.int32, (1, KV_ROWS), 1) % KV_SUB
    real = col_j < MEM_HEADS
    ts = []
    for s in range(nseq):
        q = q_ref[s * SUBLANES:(s + 1) * SUBLANES, :]
        pieces = []
        for j in range(KV_SUB):
            blk = (j % MEM_HEADS) * 2 + j // MEM_HEADS
            pieces.append(q[:, blk * LANES:(blk + 1) * LANES])
        qm = jnp.concatenate(pieces, axis=0).astype(BF16)
        kb = k_ref[s].astype(BF16)
        sc = lax.dot_general(qm, kb, (((1,), (1,)), ((), ())), preferred_element_type=F32)
        t = sc[0:SUBLANES]
        for j in range(1, KV_SUB):
            t = jnp.where(col_j == j, sc[j * SUBLANES:(j + 1) * SUBLANES], t)
        ts.append(t)
    t = jnp.concatenate(ts, axis=0)
    t = (t + pltpu.roll(t, KV_ROWS - MEM_HEADS, 1)) * scale
    t = jnp.where(real, t, jnp.finfo(F32).min)
    m = t[:, 0:LANES]
    for i in range(1, nvreg):
        m = jnp.maximum(m, t[:, i * LANES:(i + 1) * LANES])
    d = KV_SUB
    while d < LANES:
        m = jnp.maximum(m, pltpu.roll(m, d, 1))
        d *= 2
    e = jnp.where(real, jnp.exp(t - jnp.concatenate([m] * nvreg, axis=1)), 0.0)
    z = e[:, 0:LANES]
    for i in range(1, nvreg):
        z = z + e[:, i * LANES:(i + 1) * LANES]
    d = KV_SUB
    while d < LANES:
        z = z + pltpu.roll(z, d, 1)
        d *= 2
    z = jnp.where(real[:, 0:LANES], z, 1.0)
    p = e / jnp.concatenate([z] * nvreg, axis=1)
    p = p + pltpu.roll(p, MEM_HEADS, 1)
    for s in range(nseq):
        p_s = p[s * SUBLANES:(s + 1) * SUBLANES]
        pm = jnp.concatenate([jnp.where(col_j == j, p_s, 0.0) for j in range(KV_SUB)], axis=0).astype(BF16)
        o = jnp.dot(pm, v_ref[s].astype(BF16), preferred_element_type=F32)
        for blk in range(KV_SUB):
            j = (blk % 2) * MEM_HEADS + blk // 2
            o_ref[s * SUBLANES:(s + 1) * SUBLANES, blk * LANES:(blk + 1) * LANES] = (
                o[j * SUBLANES:(j + 1) * SUBLANES, :].astype(o_ref.dtype))


def _cache_attn(q, k, v, layer, *, nseq_blk=4):
    rows = q.shape[0]
    nseq = rows // SUBLANES
    assert q.shape[1] == KV_SUB * LANES and nseq % nseq_blk == 0
    q_spec = pl.BlockSpec((nseq_blk * SUBLANES, D_MODEL), lambda i: (i, 0))
    kv_spec = pl.BlockSpec((None, nseq_blk, KV_ROWS, LANES), lambda i: (layer, i, 0, 0))
    return pl.pallas_call(
        functools.partial(_cache_attn_kernel, nseq=nseq_blk),
        grid=(nseq // nseq_blk,),
        in_specs=[q_spec, kv_spec, kv_spec],
        out_specs=q_spec,
        out_shape=jax.ShapeDtypeStruct((rows, D_MODEL), BF16),
        name="cache_attn",
        compiler_params=_cparams(("parallel",)),
    )(q, k, v)


def _causal_conv(xt, carry, w_ref, b_ref, cols, *, T):
    n = xt.shape[-1]
    sub = lax.broadcasted_iota(jnp.int32, (1, SUBLANES, 1), 1)
    y = b_ref[:, cols].reshape(1, 1, n) + w_ref[CONV_WIDTH - 1:CONV_WIDTH, cols].reshape(1, 1, n) * xt
    for d in range(1, CONV_WIDTH):
        cur = pltpu.roll(xt, d, 1)
        before = pltpu.roll(carry, d, 1)
        if T != SUBLANES:
            before = jnp.concatenate([before, cur[:-1]], axis=0)
        tap = w_ref[CONV_WIDTH - 1 - d:CONV_WIDTH - d, cols].reshape(1, 1, n)
        y = y + tap * jnp.where(sub < d, before, cur)
    return y


def _conv_carry(xt, *, T, LV):
    if T != SUBLANES:
        assert LV == T
        return xt[-1:]
    assert LV >= CONV_WIDTH - 1
    return xt if LV == SUBLANES else pltpu.roll(xt, SUBLANES - LV, 1)


def _mix_chunk(x, ycat_ref, c, refs, *, S, T, NC, LV, pos0, side_work=(), late_work=()):
    pending = [iter(side_work)]

    def tick():
        task = next(pending[0], None)
        if task is not None:
            task()

    (pool_st, sconv_st, ssd_st, lconv_st, lru_st,
     pool_w, pool_scale, sconv_w, sconv_b, dt_bias, a_log, d_skip, ssd_norm,
     lconv_w, lconv_b, gate_w, gate_b, lam, expand, ssd_prev,
     out_ref, pool_out, sconv_out, ssd_out, lconv_out, lru_out,
     ext_pool, ext_sconv, ext_lconv, lru_carry, h_state) = refs
    R = S * T
    carried = NC > 1
    row = lax.broadcasted_iota(jnp.int32, (R, 1), 0)
    t_in = row % T
    pos = pos0 + c * LV + t_in

    u_pool = x[:, C_POOL:C_POOL + POOL_WIDTH]
    ext_pool[:, POOL_TILE:, :] = u_pool.reshape(S, T, POOL_WIDTH)
    for g, w in enumerate(POOL_WINDOWS):
        cols = slice(g * POOL_GDIM, (g + 1) * POOL_GDIM)
        win = ext_pool[:, :, cols].reshape(S * (POOL_TILE + T), POOL_GDIM)
        shift = 1
        while shift < w:
            win = win + pltpu.roll(win, shift, 0)
            shift *= 2
        acc = win.reshape(S, POOL_TILE + T, POOL_GDIM)[:, POOL_TILE:, :]
        cnt = jnp.minimum(pos + 1, w).astype(F32)
        pooled = acc.reshape(R, POOL_GDIM) / cnt - u_pool[:, cols]
        y = jnp.dot(pooled.astype(BF16), pool_w[g], preferred_element_type=F32)
        ycat_ref[:, cols] = (y * pool_scale[:, cols]).astype(ycat_ref.dtype)
    new_pool = ext_pool[:, pl.ds(LV, POOL_TILE), :]
    ext_pool[:, 0:POOL_TILE, :] = new_pool

    conv_blocks = []
    for j, n in _col_chunks(SSD_CONV_DIM):
        cols = slice(j, j + n)
        xt = x[:, C_XBC + j:C_XBC + j + n].reshape(R // SUBLANES, SUBLANES, n)
        conv = _causal_conv(xt, ext_sconv[:, :, cols], sconv_w, sconv_b, cols, T=T)
        ext_sconv[:, :, cols] = _conv_carry(xt, T=T, LV=LV)
        conv_blocks.append(_silu(conv).reshape(R, n))
    xbc = jnp.concatenate(conv_blocks, axis=1)

    lane = lax.broadcasted_iota(jnp.int32, (1, LANES), 1)
    dt = _softplus(x[:, C_DT:C_DT + LANES] + dt_bias[...])
    dt = jnp.where((lane < SSD_HEADS) & (t_in < LV), dt, 0.0)
    d_a = dt * (-jnp.exp(a_log[...]))
    z = x[:, C_Z:C_Z + SSD_WIDTH]

    li = lax.broadcasted_iota(jnp.int32, (SUBCHUNK, SUBCHUNK), 0)
    si = lax.broadcasted_iota(jnp.int32, (SUBCHUNK, SUBCHUNK), 1)
    same_seq = (li // T) == (si // T) if S > 1 else (li >= 0)
    causal = same_seq & (si <= li)
    causal_b = causal.astype(BF16)
    same_b = same_seq.astype(BF16)
    lane_q = lax.broadcasted_iota(jnp.int32, (SUBCHUNK, LANES), 1)
    row_q = lax.broadcasted_iota(jnp.int32, (SUBCHUNK, 1), 0)
    expand_m = expand[...]

    for q in range(R // SUBCHUNK):
        rs = slice(q * SUBCHUNK, (q + 1) * SUBCHUNK)
        xs = xbc[rs, 0:SSD_WIDTH]
        d_a_q = d_a[rs]
        acs = _sel_dot(causal_b, d_a_q)
        tot = _sel_dot(same_b, d_a_q)
        acs_row = acs.T
        dt_e = _spread(dt[rs], expand_m)
        eacs_e = _spread(jnp.exp(acs), expand_m)
        etot = jnp.exp(tot)
        etot_e = _spread(etot[0:SUBLANES] if carried else etot, expand_m)
        xdt = xs * dt_e
        xdt_b = xdt.astype(BF16)
        xdecay = xdt * _spread(jnp.exp(tot - acs), expand_m)
        xdecay_b = xdecay.astype(BF16)
        tick()
        y_parts = []
        for g in range(SSD_GROUPS):
            gc = slice(g * SSD_GWIDTH, (g + 1) * SSD_GWIDTH)
            b_g = xbc[rs, SSD_WIDTH + g * SSD_STATE:SSD_WIDTH + (g + 1) * SSD_STATE].astype(BF16)
            c_g = xbc[rs, SSD_WIDTH + (SSD_GROUPS + g) * SSD_STATE:
                      SSD_WIDTH + (SSD_GROUPS + g + 1) * SSD_STATE].astype(BF16)
            cb = lax.dot_general(c_g, b_g, (((1,), (1,)), ((), ())), preferred_element_type=F32)
            diag = []
            for j in range(SSD_GWIDTH // LANES):
                pair = []
                for hh in range(2):
                    k = g * (SSD_HEADS // SSD_GROUPS) + 2 * j + hh
                    seg = acs[:, k:k + 1] - acs_row[k:k + 1, :]
                    lmat = jnp.where(causal, jnp.exp(jnp.where(causal, seg, 0.0)), 0.0)
                    pair.append((cb * lmat).astype(BF16))
                xp = xdt_b[:, g * SSD_GWIDTH + j * LANES:g * SSD_GWIDTH + (j + 1) * LANES]
                top = jnp.where(lane_q < SSD_HEAD_DIM, xp, jnp.zeros_like(xp))
                bot = jnp.where(lane_q >= SSD_HEAD_DIM, xp, jnp.zeros_like(xp))
                diag.append(jnp.dot(jnp.concatenate(pair, axis=1), jnp.concatenate([top, bot], axis=0),
                                    preferred_element_type=F32))
            y_g = jnp.concatenate(diag, axis=1)
            if carried:
                h_t = h_state[0, g]
                y_g = y_g + jnp.dot(c_g, h_t.astype(BF16), preferred_element_type=F32) * eacs_e[:, gc]
                upd = lax.dot_general(b_g, xdecay_b[:, gc], (((0,), (0,)), ((), ())),
                                      preferred_element_type=F32)
                h_state[0, g] = h_t * etot_e[0:1, gc] + upd
            else:
                xd_t = xdecay[:, gc].T.astype(BF16)
                etot_t = etot_e[:, gc].T
                y_off = jnp.zeros((SUBCHUNK, SSD_GWIDTH), F32)
                for s in range(S):
                    mine = (row_q // T) == s
                    h_s = ssd_st[s, gc, :]
                    c_s = jnp.where(mine, c_g, jnp.zeros_like(c_g))
                    b_s = jnp.where(mine, b_g, jnp.zeros_like(b_g))
                    y_off = y_off + lax.dot_general(c_s, h_s.astype(BF16), (((1,), (1,)), ((), ())),
                                                    preferred_element_type=F32)
                    upd = jnp.dot(xd_t, b_s, preferred_element_type=F32)
                    ssd_out[s, gc, :] = h_s * etot_t[:, s * T:s * T + 1] + upd
                y_g = y_g + y_off * eacs_e[:, gc]
            y_parts.append(y_g)
            tick()
        y = jnp.concatenate(y_parts, axis=1) + xs * d_skip[...]
        y = _rms(y * _silu(z[rs]), ssd_norm[...])
        ycat_ref[rs, POOL_WIDTH:POOL_WIDTH + SSD_WIDTH] = y.astype(ycat_ref.dtype)

    pending[0] = itertools.chain(pending[0], late_work)
    xt = x[:, C_LRU:C_LRU + LRU_WIDTH].reshape(R // SUBLANES, SUBLANES, LRU_WIDTH)
    xc = _causal_conv(xt, ext_lconv[...], lconv_w, lconv_b, slice(0, LRU_WIDTH), T=T).reshape(R, LRU_WIDTH)
    ext_lconv[...] = _conv_carry(xt, T=T, LV=LV)
    tick()
    xc_b = xc.astype(BF16)
    r_lin, i_lin = [], []
    for p in range(LRU_WIDTH // LANES):
        g2 = jnp.dot(xc_b[:, p * LANES:(p + 1) * LANES], gate_w[p], preferred_element_type=F32)
        r_lin.append(g2[:, 0:LANES])
        i_lin.append(g2[:, LANES:2 * LANES])
    r_gate = jax.nn.sigmoid(jnp.concatenate(r_lin, axis=1) + gate_b[:, 0:LRU_WIDTH])
    i_gate = jax.nn.sigmoid(jnp.concatenate(i_lin, axis=1) + gate_b[:, LRU_WIDTH:2 * LRU_WIDTH])
    log_a = (-LRU_C) * r_gate * _softplus(-lam[...])
    a = jnp.exp(log_a)
    mult = jnp.where(pos == 0, 1.0, jnp.sqrt(1.0 - jnp.exp(2.0 * log_a)))
    b = mult * i_gate * xc
    tick()
    tiles = R // SUBLANES
    a3 = a.reshape(tiles, SUBLANES, LRU_WIDTH)
    b3 = b.reshape(tiles, SUBLANES, LRU_WIDTH)
    sub = lax.broadcasted_iota(jnp.int32, (1, SUBLANES, 1), 1)
    d = 1
    while d < SUBLANES:
        keep = sub >= d
        a_prev = jnp.where(keep, pltpu.roll(a3, d, 1), 1.0)
        b_prev = jnp.where(keep, pltpu.roll(b3, d, 1), 0.0)
        b3 = a3 * b_prev + b3
        a3 = a3 * a_prev
        d *= 2
        tick()
    if S == 1:
        h_prev = lru_carry[0, 0:1, :]
        h_tiles = []
        for i in range(tiles):
            h_i = a3[i] * h_prev + b3[i]
            h_tiles.append(h_i)
            h_prev = h_i[SUBLANES - 1:SUBLANES, :]
        h = jnp.concatenate(h_tiles, axis=0)
    else:
        h = (a3 * lru_carry[...] + b3).reshape(R, LRU_WIDTH)
    gate = x[:, C_GATE:C_GATE + LRU_WIDTH]
    ycat_ref[:, POOL_WIDTH + SSD_WIDTH:D_MIX] = (h * _gelu_tanh(gate)).astype(ycat_ref.dtype)
    if S == 1:
        lru_carry[0] = jnp.broadcast_to(h[LV - 1:LV, :], (SUBLANES, LRU_WIDTH))
    else:
        li_r = lax.broadcasted_iota(jnp.int32, (R, R), 0)
        si_r = lax.broadcasted_iota(jnp.int32, (R, R), 1)
        pick_last = ((li_r // T) == (si_r // T)) & ((si_r % T) == LV - 1)
        lru_carry[...] = _sel_dot(pick_last.astype(BF16), h).reshape(S, T, LRU_WIDTH)
    for task in pending[0]:
        task()


def _mix_kernel(*refs, S, T, NC, LV, pos0, fused):
    refs = list(refs)
    if fused:
        h_ref, h_next, norm_g, w_in, w_out = refs[:5]
        del refs[:5]
    else:
        x = refs.pop(0)
    io = tuple(refs[:31])
    (pool_st, sconv_st, ssd_st, lconv_st, lru_st) = io[:5]
    (out_ref, pool_out, sconv_out, ssd_out, lconv_out, lru_out,
     ext_pool, ext_sconv, ext_lconv, lru_carry, h_state) = io[20:]
    cfg = dict(S=S, T=T, NC=NC, LV=LV, pos0=pos0)
    c = pl.program_id(1)
    carried = NC > 1

    @pl.when(c == 0)
    def _load_state():
        ext_pool[:, 0:POOL_TILE, :] = pool_st[...]
        ext_sconv[:, 0:CONV_TILE, :] = sconv_st[...]
        ext_lconv[:, 0:CONV_TILE, :] = lconv_st[...]
        lru_carry[...] = lru_st[...]
        if carried:
            for g in range(SSD_GROUPS):
                h_state[0, g] = ssd_st[0, g * SSD_GWIDTH:(g + 1) * SSD_GWIDTH, :].T

    if not fused:
        _mix_chunk(x, out_ref, c, io, **cfg)
    else:
        proj_even, proj_odd, ycat_ref = refs[31:34]

        def projection_steps(src, dst):
            normed = []

            def column_step(j, n):
                if not normed:
                    normed.append(_rms(src[...], norm_g[...]).astype(BF16))
                dst[:, j:j + n] = jnp.dot(normed[0], w_in[:, j:j + n], preferred_element_type=F32)

            return [functools.partial(column_step, j, n) for j, n in _col_chunks(N_PROJ)]

        def run_chunk(cur, side_work):
            early = POOL_WIDTH + SSD_WIDTH

            def project_early_columns():
                out_ref[...] = h_ref[...] + jnp.dot(ycat_ref[:, 0:early], w_out[0:early, :],
                                                    preferred_element_type=F32)

            _mix_chunk(cur, ycat_ref, c, io, side_work=side_work, late_work=[project_early_columns], **cfg)
            out_ref[...] += jnp.dot(ycat_ref[:, early:D_MIX], w_out[early:D_MIX, :],
                                    preferred_element_type=F32)

        @pl.when(c == 0)
        def _first_projection():
            for step in projection_steps(h_ref, proj_even):
                step()

        for parity, cur, nxt in ((0, proj_even, proj_odd), (1, proj_odd, proj_even)):
            @pl.when(c % 2 == parity)
            def _chunk(cur=cur, nxt=nxt):
                run_chunk(cur, projection_steps(h_next, nxt))

    @pl.when(c == NC - 1)
    def _store_state():
        pool_out[...] = ext_pool[:, 0:POOL_TILE, :]
        sconv_out[...] = ext_sconv[:, 0:CONV_TILE, :]
        lconv_out[...] = ext_lconv[:, 0:CONV_TILE, :]
        lru_out[...] = lru_carry[...]
        if carried:
            for g in range(SSD_GROUPS):
                ssd_out[0, g * SSD_GWIDTH:(g + 1) * SSD_GWIDTH, :] = h_state[0, g].T


def _mix(x, st, ssd_state, ssd_layer, ssd_new, out_layer, w, *, S, T, NC, LV, pos0, proj=None):
    rows = x.shape[0]
    nseq = rows // (NC * T)
    R = S * T
    fused = proj is not None
    assert nseq % S == 0 and R % SUBCHUNK == 0
    assert (S == 1) or (NC == 1 and R == SUBCHUNK and T == SUBLANES)
    seq3 = lambda i, c: (i, 0, 0)
    row_blk = lambda i, c: (i * NC + c, 0)
    if fused:
        next_blk = lambda i, c: (i * NC + jnp.minimum(c + 1, NC - 1), 0)
        lead_specs = [pl.BlockSpec((R, D_MODEL), row_blk), pl.BlockSpec((R, D_MODEL), next_blk),
                      _resident((1, D_MODEL)), _layer_resident(proj[1]), _layer_resident(proj[2])]
        lead_args = [x, x, proj[0].reshape(1, D_MODEL), proj[1][0], proj[2][0]]
    else:
        lead_specs = [pl.BlockSpec((R, N_PROJ), row_blk)]
        lead_args = [x]
    in_specs = lead_specs + [
        pl.BlockSpec((S, POOL_TILE, POOL_WIDTH), seq3),
        pl.BlockSpec((S, CONV_TILE, SSD_CONV_DIM), seq3),
        pl.BlockSpec((None, S, SSD_WIDTH, SSD_STATE), lambda i, c: (ssd_layer, i, 0, 0)),
        pl.BlockSpec((S, CONV_TILE, LRU_WIDTH), seq3),
        pl.BlockSpec((S, SUBLANES, LRU_WIDTH), seq3),
    ] + [_resident(a.shape) for a in w] + [pl.BlockSpec(memory_space=pl.ANY)]
    if ssd_new is None:
        ssd_new = jnp.zeros((DEPTH, nseq, SSD_WIDTH, SSD_STATE), F32)
    aliases = {len(in_specs) - 1: 3}
    out_specs = [
        pl.BlockSpec((R, D_MODEL if fused else D_MIX), row_blk),
        pl.BlockSpec((S, POOL_TILE, POOL_WIDTH), seq3),
        pl.BlockSpec((S, CONV_TILE, SSD_CONV_DIM), seq3),
        pl.BlockSpec((None, S, SSD_WIDTH, SSD_STATE), lambda i, c: (out_layer, i, 0, 0)),
        pl.BlockSpec((S, CONV_TILE, LRU_WIDTH), seq3),
        pl.BlockSpec((S, SUBLANES, LRU_WIDTH), seq3),
    ]
    out_shape = [
        jax.ShapeDtypeStruct((rows, D_MODEL), F32) if fused else jax.ShapeDtypeStruct((rows, D_MIX), BF16),
        jax.ShapeDtypeStruct((nseq, POOL_TILE, POOL_WIDTH), F32),
        jax.ShapeDtypeStruct((nseq, CONV_TILE, SSD_CONV_DIM), F32),
        jax.ShapeDtypeStruct((DEPTH, nseq, SSD_WIDTH, SSD_STATE), F32),
        jax.ShapeDtypeStruct((nseq, CONV_TILE, LRU_WIDTH), F32),
        jax.ShapeDtypeStruct((nseq, SUBLANES, LRU_WIDTH), F32),
    ]
    h_scratch = (1, SSD_GROUPS, SSD_STATE, SSD_GWIDTH) if NC > 1 else (1, 1, SUBLANES, LANES)
    scratch = [
        pltpu.VMEM((S, POOL_TILE + T, POOL_WIDTH), F32),
        pltpu.VMEM((S, CONV_TILE, SSD_CONV_DIM), F32),
        pltpu.VMEM((S, CONV_TILE, LRU_WIDTH), F32),
        pltpu.VMEM((S, SUBLANES, LRU_WIDTH), F32),
        pltpu.VMEM(h_scratch, F32),
    ]
    if fused:
        scratch += [pltpu.VMEM((R, N_PROJ), F32), pltpu.VMEM((R, N_PROJ), F32), pltpu.VMEM((R, D_MIX), BF16)]
    return pl.pallas_call(
        functools.partial(_mix_kernel, S=S, T=T, NC=NC, LV=LV, pos0=pos0, fused=fused),
        grid=(nseq // S, NC),
        in_specs=in_specs,
        out_specs=out_specs,
        out_shape=out_shape,
        scratch_shapes=scratch,
        input_output_aliases=aliases,
        name="mix_seq%d" % S,
        compiler_params=_cparams(("parallel", "arbitrary")),
    )(*lead_args, st[0], st[1], ssd_state, st[2], st[3], *w, ssd_new)


def _expand_matrix():
    e = np.zeros((LANES, SSD_WIDTH), np.float32)
    for k in range(SSD_HEADS):
        e[k, k * SSD_HEAD_DIM:(k + 1) * SSD_HEAD_DIM] = 1.0
    return jnp.asarray(np.concatenate([e, e], axis=0), BF16)


def _pair_block_diag(w):
    pairs = w.reshape(LRU_HEADS // 2, 2, LRU_HDIM, LRU_HDIM)
    eye = jnp.eye(2, dtype=w.dtype)
    return jnp.einsum('phij,hg->phigj', pairs, eye).reshape(LRU_HEADS // 2, 2 * LRU_HDIM, 2 * LRU_HDIM)


def _pad_lanes(v):
    return jnp.pad(v, (0, LANES - v.shape[0])).reshape(1, LANES)


def _matmul_weights(P):
    named = dict(w_in=P['w_in'], w_out=P['w_out'], wq=P['w_mem_q'], wo=P['w_mem_o'], wg=P['w_ffn_gate'],
                 wu=P['w_ffn_up'], wd=P['w_ffn_down'])
    big = {k: v.astype(BF16) for k, v in named.items()}
    w_in = big['w_in']
    off_dt = 3072
    big['w_in'] = jnp.concatenate(
        [w_in[:, :, :off_dt], w_in[:, :, off_dt + SSD_HEADS:], w_in[:, :, off_dt:off_dt + SSD_HEADS],
         jnp.zeros((DEPTH, D_MODEL, N_PROJ - C_DT - SSD_HEADS), BF16)], axis=2)
    return big


def _layer_params(l, P, big):
    mix_w = (
        P['pool_w'][l].astype(BF16),
        P['pool_scale'][l].reshape(1, POOL_WIDTH),
        P['ssd_conv_w'][l],
        P['ssd_conv_b'][l].reshape(1, SSD_CONV_DIM),
        _pad_lanes(P['ssd_dt_bias'][l]),
        _pad_lanes(P['ssd_a_log'][l]),
        jnp.repeat(P['ssd_d'][l], SSD_HEAD_DIM).reshape(1, SSD_WIDTH),
        P['ssd_norm'][l].reshape(1, SSD_WIDTH),
        P['lru_conv_w'][l],
        P['lru_conv_b'][l].reshape(1, LRU_WIDTH),
        jnp.concatenate([_pair_block_diag(P['lru_wa'][l]), _pair_block_diag(P['lru_wx'][l])], axis=2).astype(BF16),
        jnp.concatenate([P['lru_ba'][l].reshape(1, LRU_WIDTH), P['lru_bx'][l].reshape(1, LRU_WIDTH)], axis=1),
        P['lru_lambda'][l].reshape(1, LRU_WIDTH),
        _expand_matrix(),
    )
    out = dict(mix=mix_w, norm_mix=P['norm_mix'][l], norm_mem=P['norm_mem'][l], norm_ffn=P['norm_ffn'][l])
    out.update({k: (v, l) for k, v in big.items()})
    return out


def _run_trunk(h, layers, norm_final, states, ssd_state, attend, *, mix_cfg, fuse_proj):
    new_states = []
    ssd_new = None
    for l, lp in enumerate(layers):
        ssd_layer = min(l, ssd_state.shape[0] - 1)
        if fuse_proj:
            h, *st = _mix(h, states[l], ssd_state, ssd_layer, ssd_new, l, lp['mix'],
                          proj=(lp['norm_mix'], lp['w_in'], lp['w_out']), **mix_cfg)
        else:
            proj = _linear(h, lp['w_in'], name="in_proj", gain=lp['norm_mix'])
            ycat, *st = _mix(proj, states[l], ssd_state, ssd_layer, ssd_new, l, lp['mix'], **mix_cfg)
            h = _linear(ycat, lp['w_out'], name="out_proj", residual=h)
        ssd_new = st[2]
        h = attend(h, lp, l)
        last = l == len(layers) - 1
        h = _ffn(h, lp['norm_ffn'], lp['wg'], lp['wu'], lp['wd'], final_gain=norm_final if last else None)
        new_states.append(st)
    return h, new_states


def _unpack_states(new_states, nseq):
    pool = jnp.stack([s[0][:, 1:POOL_TILE] for s in new_states])
    sconv = jnp.stack([s[1][:, CONV_TILE - 3:CONV_TILE] for s in new_states])
    ssd = new_states[-1][2].reshape(DEPTH, nseq, SSD_HEADS, SSD_HEAD_DIM, SSD_STATE)
    lconv = jnp.stack([s[3][:, CONV_TILE - 3:CONV_TILE] for s in new_states])
    lru = jnp.stack([s[4][:, 0] for s in new_states])
    return pool, sconv, ssd, lconv, lru


def _trunk_prompt(x_prompt, mem_prompt, P, layers, *, chunk=256, tq=512):
    batch, seq, _ = x_prompt.shape
    chunk = min(chunk, seq)
    tq = min(tq, seq)
    wk = P['w_mem_k'].astype(BF16)
    wv = P['w_mem_v'].astype(BF16)
    mem_k, mem_v = _kv_proj(mem_prompt.reshape(batch * N_MEM, D_MODEL), wk, wv)
    mem_k = mem_k.reshape(DEPTH, batch, N_MEM, D_MODEL)
    mem_v = mem_v.reshape(DEPTH, batch, N_MEM, D_MODEL)
    zeros = (jnp.zeros((batch, POOL_TILE, POOL_WIDTH), F32), jnp.zeros((batch, CONV_TILE, SSD_CONV_DIM), F32),
             jnp.zeros((batch, CONV_TILE, LRU_WIDTH), F32), jnp.zeros((batch, SUBLANES, LRU_WIDTH), F32))
    ssd0 = jnp.zeros((1, batch, SSD_WIDTH, SSD_STATE), F32)
    y, new_states = _run_trunk(
        x_prompt.reshape(batch * seq, D_MODEL), layers, P['norm_final'], [zeros] * DEPTH, ssd0,
        lambda h, lp, l: _attn_block(h, lp['norm_mem'], lp['wq'], lp['wo'], mem_k, mem_v, l, tq=tq, seq_rows=seq),
        mix_cfg=dict(S=1, T=chunk, NC=seq // chunk, LV=chunk, pos0=0), fuse_proj=True)
    return (y.reshape(batch, seq, D_MODEL),) + _unpack_states(new_states, batch) + (
        mem_k.reshape(DEPTH, batch, N_MEM, MEM_HEADS, MEM_HDIM),
        mem_v.reshape(DEPTH, batch, N_MEM, MEM_HEADS, MEM_HDIM))


def _trunk_sample(x_sample, past_len, state_pool, state_ssd_conv, state_ssd, state_lru_conv, state_lru,
                  cache_k, cache_v, P, layers, *, seq_blk=16, attn_blk=8):
    nseq, seq, _ = x_sample.shape
    assert seq <= SUBLANES
    pad_t = SUBLANES - seq
    h = jnp.pad(x_sample, ((0, 0), (0, pad_t), (0, 0))).reshape(nseq * SUBLANES, D_MODEL)
    states = []
    for l in range(DEPTH):
        states.append((
            jnp.pad(state_pool[l], ((0, 0), (POOL_TILE - POOL_BUF, 0), (0, 0))),
            jnp.pad(state_ssd_conv[l], ((0, 0), (CONV_TILE - 3, 0), (0, 0))),
            jnp.pad(state_lru_conv[l], ((0, 0), (CONV_TILE - 3, 0), (0, 0))),
            jnp.broadcast_to(state_lru[l][:, None, :], (nseq, SUBLANES, LRU_WIDTH)),
        ))
    kc, vc = _cache_view(cache_k), _cache_view(cache_v)

    def attend(h, lp, l):
        q = _linear(h, lp['wq'], name="q_proj", gain=lp['norm_mem'])
        return _linear(_cache_attn(q, kc, vc, l, nseq_blk=attn_blk), lp['wo'], name="o_proj", residual=h)

    y, new_states = _run_trunk(
        h, layers, P['norm_final'], states, state_ssd.reshape(DEPTH, nseq, SSD_WIDTH, SSD_STATE), attend,
        mix_cfg=dict(S=seq_blk, T=SUBLANES, NC=1, LV=seq, pos0=past_len), fuse_proj=False)
    y = y.reshape(nseq, SUBLANES, D_MODEL)[:, :seq]
    return (y,) + _unpack_states(new_states, nseq)


def kernel(x_prompt, x_sample, mem_prompt, state_pool, state_ssd_conv, state_ssd, state_lru_conv, state_lru,
           cache_mem_k, cache_mem_v, norm_mix, w_in, pool_w, pool_scale, ssd_conv_w, ssd_conv_b, ssd_dt_bias,
           ssd_a_log, ssd_d, ssd_norm, lru_conv_w, lru_conv_b, lru_wa, lru_ba, lru_wx, lru_bx, lru_lambda,
           w_out, norm_mem, w_mem_q, w_mem_k, w_mem_v, w_mem_o, norm_ffn, w_ffn_gate, w_ffn_up, w_ffn_down,
           norm_final):
    P = dict(norm_mix=norm_mix, w_in=w_in, pool_w=pool_w, pool_scale=pool_scale, ssd_conv_w=ssd_conv_w,
             ssd_conv_b=ssd_conv_b, ssd_dt_bias=ssd_dt_bias, ssd_a_log=ssd_a_log, ssd_d=ssd_d, ssd_norm=ssd_norm,
             lru_conv_w=lru_conv_w, lru_conv_b=lru_conv_b, lru_wa=lru_wa, lru_ba=lru_ba, lru_wx=lru_wx,
             lru_bx=lru_bx, lru_lambda=lru_lambda, w_out=w_out, norm_mem=norm_mem, w_mem_q=w_mem_q,
             w_mem_k=w_mem_k, w_mem_v=w_mem_v, w_mem_o=w_mem_o, norm_ffn=norm_ffn, w_ffn_gate=w_ffn_gate,
             w_ffn_up=w_ffn_up, w_ffn_down=w_ffn_down, norm_final=norm_final)
    big = _matmul_weights(P)
    layers = [_layer_params(l, P, big) for l in range(DEPTH)]
    (y_prompt, p_pool, p_sconv, p_ssd, p_lconv, p_lru, p_mem_k, p_mem_v) = _trunk_prompt(
        x_prompt, mem_prompt, P, layers)
    (y_sample, s_pool, s_sconv, s_ssd, s_lconv, s_lru) = _trunk_sample(
        x_sample, PAST_LEN, state_pool, state_ssd_conv, state_ssd, state_lru_conv, state_lru,
        cache_mem_k, cache_mem_v, P, layers)
    return (y_prompt, y_sample, p_pool, p_sconv, p_ssd, p_lconv, p_lru, p_mem_k, p_mem_v,
            s_pool, s_sconv, s_ssd, s_lconv, s_lru)
```

```python
import functools
import itertools
import math

import numpy as np
import jax
import jax.numpy as jnp
from jax import lax
from jax.experimental import pallas as pl
from jax.experimental.pallas import tpu as pltpu

F32 = jnp.float32
BF16 = jnp.bfloat16

D_MODEL = 1024
DEPTH = 4
EPS = 1e-6
POOL_WIDTH = 512
POOL_WINDOWS = (2, 4, 8, 16)
POOL_GDIM = 128
POOL_BUF = 15
SSD_WIDTH = 1024
SSD_HEAD_DIM = 64
SSD_HEADS = 16
SSD_GROUPS = 2
SSD_GWIDTH = SSD_WIDTH // SSD_GROUPS
SSD_STATE = 128
SSD_CONV_DIM = 1536
CONV_WIDTH = 4
LRU_WIDTH = 512
LRU_HEADS = 8
LRU_HDIM = 64
LRU_C = 8.0
N_MEM = 256
MEM_HEADS = 4
MEM_HDIM = 256
D_FF = 2816
D_MIX = 2048
PAST_LEN = 16384

C_POOL = 0
C_Z = 512
C_XBC = 1536
C_GATE = 3072
C_LRU = 3584
C_DT = 4096
N_PROJ = 4224

LANES = 128
SUBLANES = 8
SUBCHUNK = 128
CONV_TILE = SUBLANES
POOL_TILE = 2 * SUBLANES
VMEM_LIMIT = 56 * 1024 * 1024


def _cparams(sem):
    return pltpu.CompilerParams(dimension_semantics=sem, vmem_limit_bytes=VMEM_LIMIT)


def _resident(shape):
    nd = len(shape)
    return pl.BlockSpec(shape, lambda *_: (0,) * nd, pipeline_mode=pl.Buffered(1))


def _layer_resident(w):
    stacked, layer = w
    nd = stacked.ndim - 1
    return pl.BlockSpec((None,) + stacked.shape[1:], lambda *_: (layer,) + (0,) * nd,
                        pipeline_mode=pl.Buffered(1))


def _rms(x, g):
    var = jnp.mean(x * x, axis=-1, keepdims=True)
    return x * lax.rsqrt(var + EPS) * g


def _silu(x):
    return x * jax.nn.sigmoid(x)


def _softplus(x):
    return jnp.maximum(x, 0.0) + jnp.log1p(jnp.exp(-jnp.abs(x)))


def _gelu_tanh(x):
    c = math.sqrt(2.0 / math.pi)
    return x * (0.5 * (1.0 + jnp.tanh(c * (x + 0.044715 * (x * x * x)))))


def _split3(x):
    x1 = x.astype(BF16)
    r1 = x - x1.astype(F32)
    x2 = r1.astype(BF16)
    x3 = (r1 - x2.astype(F32)).astype(BF16)
    return x1, x2, x3


def _sel_dot(sel, x):
    x1, x2, x3 = _split3(x)
    d = functools.partial(jnp.dot, preferred_element_type=F32)
    return d(sel, x1) + d(sel, x2) + d(sel, x3)


def _spread(x, sel2):
    hi = x.astype(BF16)
    lo = (x - hi.astype(F32)).astype(BF16)
    return jnp.dot(jnp.concatenate([hi, lo], axis=1), sel2, preferred_element_type=F32)


def _col_chunks(n, step=512):
    return [(j, min(step, n - j)) for j in range(0, n, step)]


def _linear_kernel(*refs, has_gain, has_res):
    x_ref, w_ref = refs[0], refs[1]
    k = 2
    g_ref = r_ref = None
    if has_gain:
        g_ref = refs[k]; k += 1
    if has_res:
        r_ref = refs[k]; k += 1
    o_ref = refs[k]
    x = x_ref[...]
    if has_gain:
        x = _rms(x.astype(F32), g_ref[...])
    xb = x.astype(BF16)
    for j, n in _col_chunks(o_ref.shape[1]):
        y = jnp.dot(xb, w_ref[:, j:j + n], preferred_element_type=F32)
        if has_res:
            y = y + r_ref[:, j:j + n]
        o_ref[:, j:j + n] = y.astype(o_ref.dtype)


def _linear(x, w, *, name, gain=None, residual=None, out_dtype=F32, tm=512):
    rows, kdim = x.shape
    n = w[0].shape[2]
    tm = min(tm, rows)
    assert rows % tm == 0
    in_specs = [pl.BlockSpec((tm, kdim), lambda i: (i, 0)), _layer_resident(w)]
    args = [x, w[0]]
    if gain is not None:
        in_specs.append(_resident((1, kdim)))
        args.append(gain.reshape(1, kdim))
    if residual is not None:
        in_specs.append(pl.BlockSpec((tm, n), lambda i: (i, 0)))
        args.append(residual)
    return pl.pallas_call(
        functools.partial(_linear_kernel, has_gain=gain is not None, has_res=residual is not None),
        grid=(rows // tm,),
        in_specs=in_specs,
        out_specs=pl.BlockSpec((tm, n), lambda i: (i, 0)),
        out_shape=jax.ShapeDtypeStruct((rows, n), out_dtype),
        name=name,
        compiler_params=_cparams(("parallel",)),
    )(*args)


def _kv_kernel(x_ref, wk_ref, wv_ref, k_ref, v_ref):
    xb = x_ref[...].astype(BF16)
    k_ref[...] = jnp.dot(xb, wk_ref[...], preferred_element_type=F32)
    v_ref[...] = jnp.dot(xb, wv_ref[...], preferred_element_type=F32)


def _kv_proj(mem2d, wk, wv, tm=512):
    rows = mem2d.shape[0]
    tm = min(tm, rows)
    wspec = pl.BlockSpec((None, D_MODEL, D_MODEL), lambda l, i: (l, 0, 0))
    ospec = pl.BlockSpec((None, tm, D_MODEL), lambda l, i: (l, i, 0))
    oshape = jax.ShapeDtypeStruct((DEPTH, rows, D_MODEL), F32)
    return pl.pallas_call(
        _kv_kernel,
        grid=(DEPTH, rows // tm),
        in_specs=[pl.BlockSpec((tm, D_MODEL), lambda l, i: (i, 0)), wspec, wspec],
        out_specs=[ospec, ospec],
        out_shape=[oshape, oshape],
        name="kv_proj",
        compiler_params=_cparams(("parallel", "parallel")),
    )(mem2d, wk, wv)


def _ffn_kernel(x_ref, g_ref, wg_ref, wu_ref, wd_ref, *rest):
    o_ref = rest[-1]
    x = x_ref[...]
    ub = _rms(x, g_ref[...]).astype(BF16)
    acc = x
    for j, n in _col_chunks(D_FF, 1024):
        gate = jnp.dot(ub, wg_ref[:, j:j + n], preferred_element_type=F32)
        up = jnp.dot(ub, wu_ref[:, j:j + n], preferred_element_type=F32)
        act = (_silu(gate) * up).astype(BF16)
        acc = acc + jnp.dot(act, wd_ref[j:j + n, :], preferred_element_type=F32)
    if len(rest) == 2:
        acc = _rms(acc, rest[0][...])
    o_ref[...] = acc


def _ffn(h, gain, wg, wu, wd, final_gain=None, tm=1024):
    rows = h.shape[0]
    tm = min(tm, rows)
    row_spec = pl.BlockSpec((tm, D_MODEL), lambda i: (i, 0))
    in_specs = [row_spec, _resident((1, D_MODEL)), _layer_resident(wg), _layer_resident(wu), _layer_resident(wd)]
    args = [h, gain.reshape(1, D_MODEL), wg[0], wu[0], wd[0]]
    if final_gain is not None:
        in_specs.append(_resident((1, D_MODEL)))
        args.append(final_gain.reshape(1, D_MODEL))
    return pl.pallas_call(
        _ffn_kernel,
        grid=(rows // tm,),
        in_specs=in_specs,
        out_specs=row_spec,
        out_shape=jax.ShapeDtypeStruct((rows, D_MODEL), F32),
        name="ffn",
        compiler_params=_cparams(("parallel",)),
    )(*args)


def _attn_block_kernel(h_ref, g_ref, wq_ref, wo_ref, k_ref, v_ref, o_ref, kb_ref, vb_ref):
    scale = MEM_HDIM ** -0.5

    @pl.when(pl.program_id(1) == 0)
    def _cast_memory():
        kb_ref[...] = k_ref[0].astype(BF16)
        vb_ref[...] = v_ref[0].astype(BF16)

    x = h_ref[...]
    q = jnp.dot(_rms(x, g_ref[...]).astype(BF16), wq_ref[...], preferred_element_type=F32).astype(BF16)
    head_cols = [slice(hd * MEM_HDIM, (hd + 1) * MEM_HDIM) for hd in range(MEM_HEADS)]
    scores = [lax.dot_general(q[:, cols], kb_ref[:, cols], (((1,), (1,)), ((), ())),
                              preferred_element_type=F32) * scale for cols in head_cols]
    probs = []
    for sc in scores:
        e = jnp.exp(sc - jnp.max(sc, axis=-1, keepdims=True))
        probs.append((e / jnp.sum(e, axis=-1, keepdims=True)).astype(BF16))
    heads = [jnp.dot(p, vb_ref[:, cols], preferred_element_type=F32).astype(BF16)
             for p, cols in zip(probs, head_cols)]
    o = jnp.concatenate(heads, axis=1)
    o_ref[...] = x + jnp.dot(o, wo_ref[...], preferred_element_type=F32)


def _attn_block(h, gain, wq, wo, k, v, layer, *, tq, seq_rows):
    rows = h.shape[0]
    nseq = rows // seq_rows
    qblocks = seq_rows // tq
    h_spec = pl.BlockSpec((tq, D_MODEL), lambda i, c: (i * qblocks + c, 0))
    kv_spec = pl.BlockSpec((None, 1, N_MEM, D_MODEL), lambda i, c: (layer, i, 0, 0))
    return pl.pallas_call(
        _attn_block_kernel,
        grid=(nseq, qblocks),
        in_specs=[h_spec, _resident((1, D_MODEL)), _layer_resident(wq), _layer_resident(wo), kv_spec, kv_spec],
        out_specs=h_spec,
        out_shape=jax.ShapeDtypeStruct((rows, D_MODEL), F32),
        scratch_shapes=[pltpu.VMEM((N_MEM, D_MODEL), BF16), pltpu.VMEM((N_MEM, D_MODEL), BF16)],
        name="attn_block",
        compiler_params=_cparams(("parallel", "arbitrary")),
    )(h, gain.reshape(1, D_MODEL), wq[0], wo[0], k, v)


KV_SUB = 2 * MEM_HEADS
KV_ROWS = N_MEM * KV_SUB


def _cache_view(c):
    d, n = c.shape[:2]
    c = c.reshape(d, n, N_MEM, MEM_HEADS, MEM_HDIM // LANES, LANES)
    return jnp.transpose(c, (0, 1, 2, 4, 3, 5)).reshape(d, n, KV_ROWS, LANES)


def _cache_attn_kernel(q_ref, k_ref, v_ref, o_ref, *, nseq):
    scale = MEM_HDIM ** -0.5
    nvreg = KV_ROWS // LANES
    col_j = lax.broadcasted_iota(jnp.int32, (1, KV_ROWS), 1) % KV_SUB
    real = col_j < MEM_HEADS
    ts = []
    for s in range(nseq):
        q = q_ref[s * SUBLANES:(s + 1) * SUBLANES, :]
        pieces = []
        for j in range(KV_SUB):
            blk = (j % MEM_HEADS) * 2 + j // MEM_HEADS
            pieces.append(q[:, blk * LANES:(blk + 1) * LANES])
        qm = jnp.concatenate(pieces, axis=0).astype(BF16)
        kb = k_ref[s].astype(BF16)
        sc = lax.dot_general(qm, kb, (((1,), (1,)), ((), ())), preferred_element_type=F32)
        t = sc[0:SUBLANES]
        for j in range(1, KV_SUB):
            t = jnp.where(col_j == j, sc[j * SUBLANES:(j + 1) * SUBLANES], t)
        ts.append(t)
    t = jnp.concatenate(ts, axis=0)
    t = (t + pltpu.roll(t, KV_ROWS - MEM_HEADS, 1)) * scale
    t = jnp.where(real, t, jnp.finfo(F32).min)
    m = t[:, 0:LANES]
    for i in range(1, nvreg):
        m = jnp.maximum(m, t[:, i * LANES:(i + 1) * LANES])
    d = KV_SUB
    while d < LANES:
        m = jnp.maximum(m, pltpu.roll(m, d, 1))
        d *= 2
    e = jnp.where(real, jnp.exp(t - jnp.concatenate([m] * nvreg, axis=1)), 0.0)
    z = e[:, 0:LANES]
    for i in range(1, nvreg):
        z = z + e[:, i * LANES:(i + 1) * LANES]
    d = KV_SUB
    while d < LANES:
        z = z + pltpu.roll(z, d, 1)
        d *= 2
    z = jnp.where(real[:, 0:LANES], z, 1.0)
    p = e / jnp.concatenate([z] * nvreg, axis=1)
    p = p + pltpu.roll(p, MEM_HEADS, 1)
    for s in range(nseq):
        p_s = p[s * SUBLANES:(s + 1) * SUBLANES]
        pm = jnp.concatenate([jnp.where(col_j == j, p_s, 0.0) for j in range(KV_SUB)], axis=0).astype(BF16)
        o = jnp.dot(pm, v_ref[s].astype(BF16), preferred_element_type=F32)
        for blk in range(KV_SUB):
            j = (blk % 2) * MEM_HEADS + blk // 2
            o_ref[s * SUBLANES:(s + 1) * SUBLANES, blk * LANES:(blk + 1) * LANES] = (
                o[j * SUBLANES:(j + 1) * SUBLANES, :].astype(o_ref.dtype))


def _cache_attn(q, k, v, layer, *, nseq_blk=4):
    rows = q.shape[0]
    nseq = rows // SUBLANES
    assert q.shape[1] == KV_SUB * LANES and nseq % nseq_blk == 0
    q_spec = pl.BlockSpec((nseq_blk * SUBLANES, D_MODEL), lambda i: (i, 0))
    kv_spec = pl.BlockSpec((None, nseq_blk, KV_ROWS, LANES), lambda i: (layer, i, 0, 0))
    return pl.pallas_call(
        functools.partial(_cache_attn_kernel, nseq=nseq_blk),
        grid=(nseq // nseq_blk,),
        in_specs=[q_spec, kv_spec, kv_spec],
        out_specs=q_spec,
        out_shape=jax.ShapeDtypeStruct((rows, D_MODEL), BF16),
        name="cache_attn",
        compiler_params=_cparams(("parallel",)),
    )(q, k, v)


def _causal_conv(xt, carry, w_ref, b_ref, cols, *, T):
    n = xt.shape[-1]
    sub = lax.broadcasted_iota(jnp.int32, (1, SUBLANES, 1), 1)
    y = b_ref[:, cols].reshape(1, 1, n) + w_ref[CONV_WIDTH - 1:CONV_WIDTH, cols].reshape(1, 1, n) * xt
    for d in range(1, CONV_WIDTH):
        cur = pltpu.roll(xt, d, 1)
        before = pltpu.roll(carry, d, 1)
        if T != SUBLANES:
            before = jnp.concatenate([before, cur[:-1]], axis=0)
        tap = w_ref[CONV_WIDTH - 1 - d:CONV_WIDTH - d, cols].reshape(1, 1, n)
        y = y + tap * jnp.where(sub < d, before, cur)
    return y


def _conv_carry(xt, *, T, LV):
    if T != SUBLANES:
        assert LV == T
        return xt[-1:]
    assert LV >= CONV_WIDTH - 1
    return xt if LV == SUBLANES else pltpu.roll(xt, SUBLANES - LV, 1)


def _mix_chunk(x, ycat_ref, c, refs, *, S, T, NC, LV, pos0, side_work=(), late_work=()):
    pending = [iter(side_work)]

    def tick():
        task = next(pending[0], None)
        if task is not None:
            task()

    (pool_st, sconv_st, ssd_st, lconv_st, lru_st,
     pool_w, pool_scale, sconv_w, sconv_b, dt_bias, a_log, d_skip, ssd_norm,
     lconv_w, lconv_b, gate_w, gate_b, lam, expand, ssd_prev,
     out_ref, pool_out, sconv_out, ssd_out, lconv_out, lru_out,
     ext_pool, ext_sconv, ext_lconv, lru_carry, h_state) = refs
    R = S * T
    carried = NC > 1
    row = lax.broadcasted_iota(jnp.int32, (R, 1), 0)
    t_in = row % T
    pos = pos0 + c * LV + t_in

    u_pool = x[:, C_POOL:C_POOL + POOL_WIDTH]
    ext_pool[:, POOL_TILE:, :] = u_pool.reshape(S, T, POOL_WIDTH)
    for g, w in enumerate(POOL_WINDOWS):
        cols = slice(g * POOL_GDIM, (g + 1) * POOL_GDIM)
        win = ext_pool[:, :, cols].reshape(S * (POOL_TILE + T), POOL_GDIM)
        shift = 1
        while shift < w:
            win = win + pltpu.roll(win, shift, 0)
            shift *= 2
        acc = win.reshape(S, POOL_TILE + T, POOL_GDIM)[:, POOL_TILE:, :]
        cnt = jnp.minimum(pos + 1, w).astype(F32)
        pooled = acc.reshape(R, POOL_GDIM) / cnt - u_pool[:, cols]
        y = jnp.dot(pooled.astype(BF16), pool_w[g], preferred_element_type=F32)
        ycat_ref[:, cols] = (y * pool_scale[:, cols]).astype(ycat_ref.dtype)
    new_pool = ext_pool[:, pl.ds(LV, POOL_TILE), :]
    ext_pool[:, 0:POOL_TILE, :] = new_pool

    conv_blocks = []
    for j, n in _col_chunks(SSD_CONV_DIM):
        cols = slice(j, j + n)
        xt = x[:, C_XBC + j:C_XBC + j + n].reshape(R // SUBLANES, SUBLANES, n)
        conv = _causal_conv(xt, ext_sconv[:, :, cols], sconv_w, sconv_b, cols, T=T)
        ext_sconv[:, :, cols] = _conv_carry(xt, T=T, LV=LV)
        conv_blocks.append(_silu(conv).reshape(R, n))
    xbc = jnp.concatenate(conv_blocks, axis=1)

    lane = lax.broadcasted_iota(jnp.int32, (1, LANES), 1)
    dt = _softplus(x[:, C_DT:C_DT + LANES] + dt_bias[...])
    dt = jnp.where((lane < SSD_HEADS) & (t_in < LV), dt, 0.0)
    d_a = dt * (-jnp.exp(a_log[...]))
    z = x[:, C_Z:C_Z + SSD_WIDTH]

    li = lax.broadcasted_iota(jnp.int32, (SUBCHUNK, SUBCHUNK), 0)
    si = lax.broadcasted_iota(jnp.int32, (SUBCHUNK, SUBCHUNK), 1)
    same_seq = (li // T) == (si // T) if S > 1 else (li >= 0)
    causal = same_seq & (si <= li)
    causal_b = causal.astype(BF16)
    same_b = same_seq.astype(BF16)
    lane_q = lax.broadcasted_iota(jnp.int32, (SUBCHUNK, LANES), 1)
    row_q = lax.broadcasted_iota(jnp.int32, (SUBCHUNK, 1), 0)
    expand_m = expand[...]

    for q in range(R // SUBCHUNK):
        rs = slice(q * SUBCHUNK, (q + 1) * SUBCHUNK)
        xs = xbc[rs, 0:SSD_WIDTH]
        d_a_q = d_a[rs]
        acs = _sel_dot(causal_b, d_a_q)
        tot = _sel_dot(same_b, d_a_q)
        acs_row = acs.T
        dt_e = _spread(dt[rs], expand_m)
        eacs_e = _spread(jnp.exp(acs), expand_m)
        etot = jnp.exp(tot)
        etot_e = _spread(etot[0:SUBLANES] if carried else etot, expand_m)
        xdt = xs * dt_e
        xdt_b = xdt.astype(BF16)
        xdecay = xdt * _spread(jnp.exp(tot - acs), expand_m)
        xdecay_b = xdecay.astype(BF16)
        tick()
        y_parts = []
        for g in range(SSD_GROUPS):
            gc = slice(g * SSD_GWIDTH, (g + 1) * SSD_GWIDTH)
            b_g = xbc[rs, SSD_WIDTH + g * SSD_STATE:SSD_WIDTH + (g + 1) * SSD_STATE].astype(BF16)
            c_g = xbc[rs, SSD_WIDTH + (SSD_GROUPS + g) * SSD_STATE:
                      SSD_WIDTH + (SSD_GROUPS + g + 1) * SSD_STATE].astype(BF16)
            cb = lax.dot_general(c_g, b_g, (((1,), (1,)), ((), ())), preferred_element_type=F32)
            diag = []
            for j in range(SSD_GWIDTH // LANES):
                pair = []
                for hh in range(2):
                    k = g * (SSD_HEADS // SSD_GROUPS) + 2 * j + hh
                    seg = acs[:, k:k + 1] - acs_row[k:k + 1, :]
                    lmat = jnp.where(causal, jnp.exp(jnp.where(causal, seg, 0.0)), 0.0)
                    pair.append((cb * lmat).astype(BF16))
                xp = xdt_b[:, g * SSD_GWIDTH + j * LANES:g * SSD_GWIDTH + (j + 1) * LANES]
                top = jnp.where(lane_q < SSD_HEAD_DIM, xp, jnp.zeros_like(xp))
                bot = jnp.where(lane_q >= SSD_HEAD_DIM, xp, jnp.zeros_like(xp))
                diag.append(jnp.dot(jnp.concatenate(pair, axis=1), jnp.concatenate([top, bot], axis=0),
                                    preferred_element_type=F32))
            y_g = jnp.concatenate(diag, axis=1)
            if carried:
                h_t = h_state[0, g]
                y_g = y_g + jnp.dot(c_g, h_t.astype(BF16), preferred_element_type=F32) * eacs_e[:, gc]
                upd = lax.dot_general(b_g, xdecay_b[:, gc], (((0,), (0,)), ((), ())),
                                      preferred_element_type=F32)
                h_state[0, g] = h_t * etot_e[0:1, gc] + upd
            else:
                xd_t = xdecay[:, gc].T.astype(BF16)
                etot_t = etot_e[:, gc].T
                y_off = jnp.zeros((SUBCHUNK, SSD_GWIDTH), F32)
                for s in range(S):
                    mine = (row_q // T) == s
                    h_s = ssd_st[s, gc, :]
                    c_s = jnp.where(mine, c_g, jnp.zeros_like(c_g))
                    b_s = jnp.where(mine, b_g, jnp.zeros_like(b_g))
                    y_off = y_off + lax.dot_general(c_s, h_s.astype(BF16), (((1,), (1,)), ((), ())),
                                                    preferred_element_type=F32)
                    upd = jnp.dot(xd_t, b_s, preferred_element_type=F32)
                    ssd_out[s, gc, :] = h_s * etot_t[:, s * T:s * T + 1] + upd
                y_g = y_g + y_off * eacs_e[:, gc]
            y_parts.append(y_g)
            tick()
        y = jnp.concatenate(y_parts, axis=1) + xs * d_skip[...]
        y = _rms(y * _silu(z[rs]), ssd_norm[...])
        ycat_ref[rs, POOL_WIDTH:POOL_WIDTH + SSD_WIDTH] = y.astype(ycat_ref.dtype)

    pending[0] = itertools.chain(pending[0], late_work)
    xt = x[:, C_LRU:C_LRU + LRU_WIDTH].reshape(R // SUBLANES, SUBLANES, LRU_WIDTH)
    xc = _causal_conv(xt, ext_lconv[...], lconv_w, lconv_b, slice(0, LRU_WIDTH), T=T).reshape(R, LRU_WIDTH)
    ext_lconv[...] = _conv_carry(xt, T=T, LV=LV)
    tick()
    xc_b = xc.astype(BF16)
    r_lin, i_lin = [], []
    for p in range(LRU_WIDTH // LANES):
        g2 = jnp.dot(xc_b[:, p * LANES:(p + 1) * LANES], gate_w[p], preferred_element_type=F32)
        r_lin.append(g2[:, 0:LANES])
        i_lin.append(g2[:, LANES:2 * LANES])
    r_gate = jax.nn.sigmoid(jnp.concatenate(r_lin, axis=1) + gate_b[:, 0:LRU_WIDTH])
    i_gate = jax.nn.sigmoid(jnp.concatenate(i_lin, axis=1) + gate_b[:, LRU_WIDTH:2 * LRU_WIDTH])
    log_a = (-LRU_C) * r_gate * _softplus(-lam[...])
    a = jnp.exp(log_a)
    mult = jnp.where(pos == 0, 1.0, jnp.sqrt(1.0 - jnp.exp(2.0 * log_a)))
    b = mult * i_gate * xc
    tick()
    tiles = R // SUBLANES
    a3 = a.reshape(tiles, SUBLANES, LRU_WIDTH)
    b3 = b.reshape(tiles, SUBLANES, LRU_WIDTH)
    sub = lax.broadcasted_iota(jnp.int32, (1, SUBLANES, 1), 1)
    d = 1
    while d < SUBLANES:
        keep = sub >= d
        a_prev = jnp.where(keep, pltpu.roll(a3, d, 1), 1.0)
        b_prev = jnp.where(keep, pltpu.roll(b3, d, 1), 0.0)
        b3 = a3 * b_prev + b3
        a3 = a3 * a_prev
        d *= 2
        tick()
    if S == 1:
        h_prev = lru_carry[0, 0:1, :]
        h_tiles = []
        for i in range(tiles):
            h_i = a3[i] * h_prev + b3[i]
            h_tiles.append(h_i)
            h_prev = h_i[SUBLANES - 1:SUBLANES, :]
        h = jnp.concatenate(h_tiles, axis=0)
    else:
        h = (a3 * lru_carry[...] + b3).reshape(R, LRU_WIDTH)
    gate = x[:, C_GATE:C_GATE + LRU_WIDTH]
    ycat_ref[:, POOL_WIDTH + SSD_WIDTH:D_MIX] = (h * _gelu_tanh(gate)).astype(ycat_ref.dtype)
    if S == 1:
        lru_carry[0] = jnp.broadcast_to(h[LV - 1:LV, :], (SUBLANES, LRU_WIDTH))
    else:
        li_r = lax.broadcasted_iota(jnp.int32, (R, R), 0)
        si_r = lax.broadcasted_iota(jnp.int32, (R, R), 1)
        pick_last = ((li_r // T) == (si_r // T)) & ((si_r % T) == LV - 1)
        lru_carry[...] = _sel_dot(pick_last.astype(BF16), h).reshape(S, T, LRU_WIDTH)
    for task in pending[0]:
        task()


def _mix_kernel(*refs, S, T, NC, LV, pos0, fused):
    refs = list(refs)
    if fused:
        h_ref, h_next, norm_g, w_in, w_out = refs[:5]
        del refs[:5]
    else:
        x = refs.pop(0)
    io = tuple(refs[:31])
    (pool_st, sconv_st, ssd_st, lconv_st, lru_st) = io[:5]
    (out_ref, pool_out, sconv_out, ssd_out, lconv_out, lru_out,
     ext_pool, ext_sconv, ext_lconv, lru_carry, h_state) = io[20:]
    cfg = dict(S=S, T=T, NC=NC, LV=LV, pos0=pos0)
    c = pl.program_id(1)
    carried = NC > 1

    @pl.when(c == 0)
    def _load_state():
        ext_pool[:, 0:POOL_TILE, :] = pool_st[...]
        ext_sconv[:, 0:CONV_TILE, :] = sconv_st[...]
        ext_lconv[:, 0:CONV_TILE, :] = lconv_st[...]
        lru_carry[...] = lru_st[...]
        if carried:
            for g in range(SSD_GROUPS):
                h_state[0, g] = ssd_st[0, g * SSD_GWIDTH:(g + 1) * SSD_GWIDTH, :].T

    if not fused:
        _mix_chunk(x, out_ref, c, io, **cfg)
    else:
        proj_even, proj_odd, ycat_ref = refs[31:34]

        def projection_steps(src, dst):
            normed = []

            def column_step(j, n):
                if not normed:
                    normed.append(_rms(src[...], norm_g[...]).astype(BF16))
                dst[:, j:j + n] = jnp.dot(normed[0], w_in[:, j:j + n], preferred_element_type=F32)

            return [functools.partial(column_step, j, n) for j, n in _col_chunks(N_PROJ)]

        def run_chunk(cur, side_work):
            early = POOL_WIDTH + SSD_WIDTH

            def project_early_columns():
                out_ref[...] = h_ref[...] + jnp.dot(ycat_ref[:, 0:early], w_out[0:early, :],
                                                    preferred_element_type=F32)

            _mix_chunk(cur, ycat_ref, c, io, side_work=side_work, late_work=[project_early_columns], **cfg)
            out_ref[...] += jnp.dot(ycat_ref[:, early:D_MIX], w_out[early:D_MIX, :],
                                    preferred_element_type=F32)

        @pl.when((c == 0) & (pl.program_id(0) == 0))
        def _first_projection():
            for step in projection_steps(h_ref, proj_even):
                step()

        for parity, cur, nxt in ((0, proj_even, proj_odd), (1, proj_odd, proj_even)):
            @pl.when(c % 2 == parity)
            def _chunk(cur=cur, nxt=nxt):
                run_chunk(cur, projection_steps(h_next, nxt))

    @pl.when(c == NC - 1)
    def _store_state():
        pool_out[...] = ext_pool[:, 0:POOL_TILE, :]
        sconv_out[...] = ext_sconv[:, 0:CONV_TILE, :]
        lconv_out[...] = ext_lconv[:, 0:CONV_TILE, :]
        lru_out[...] = lru_carry[...]
        if carried:
            for g in range(SSD_GROUPS):
                ssd_out[0, g * SSD_GWIDTH:(g + 1) * SSD_GWIDTH, :] = h_state[0, g].T


def _mix(x, st, ssd_state, ssd_layer, ssd_new, out_layer, w, *, S, T, NC, LV, pos0, proj=None):
    rows = x.shape[0]
    nseq = rows // (NC * T)
    R = S * T
    fused = proj is not None
    assert nseq % S == 0 and R % SUBCHUNK == 0
    assert (S == 1) or (NC == 1 and R == SUBCHUNK and T == SUBLANES)
    seq3 = lambda i, c: (i, 0, 0)
    row_blk = lambda i, c: (i * NC + c, 0)
    if fused:
        assert NC % 2 == 0
        next_blk = lambda i, c: (jnp.minimum(i * NC + c + 1, nseq * NC - 1), 0)
        lead_specs = [pl.BlockSpec((R, D_MODEL), row_blk), pl.BlockSpec((R, D_MODEL), next_blk),
                      _resident((1, D_MODEL)), _layer_resident(proj[1]), _layer_resident(proj[2])]
        lead_args = [x, x, proj[0].reshape(1, D_MODEL), proj[1][0], proj[2][0]]
    else:
        lead_specs = [pl.BlockSpec((R, N_PROJ), row_blk)]
        lead_args = [x]
    in_specs = lead_specs + [
        pl.BlockSpec((S, POOL_TILE, POOL_WIDTH), seq3),
        pl.BlockSpec((S, CONV_TILE, SSD_CONV_DIM), seq3),
        pl.BlockSpec((None, S, SSD_WIDTH, SSD_STATE), lambda i, c: (ssd_layer, i, 0, 0)),
        pl.BlockSpec((S, CONV_TILE, LRU_WIDTH), seq3),
        pl.BlockSpec((S, SUBLANES, LRU_WIDTH), seq3),
    ] + [_resident(a.shape) for a in w] + [pl.BlockSpec(memory_space=pl.ANY)]
    if ssd_new is None:
        ssd_new = jnp.zeros((DEPTH, nseq, SSD_WIDTH, SSD_STATE), F32)
    aliases = {len(in_specs) - 1: 3}
    out_specs = [
        pl.BlockSpec((R, D_MODEL if fused else D_MIX), row_blk),
        pl.BlockSpec((S, POOL_TILE, POOL_WIDTH), seq3),
        pl.BlockSpec((S, CONV_TILE, SSD_CONV_DIM), seq3),
        pl.BlockSpec((None, S, SSD_WIDTH, SSD_STATE), lambda i, c: (out_layer, i, 0, 0)),
        pl.BlockSpec((S, CONV_TILE, LRU_WIDTH), seq3),
        pl.BlockSpec((S, SUBLANES, LRU_WIDTH), seq3),
    ]
    out_shape = [
        jax.ShapeDtypeStruct((rows, D_MODEL), F32) if fused else jax.ShapeDtypeStruct((rows, D_MIX), BF16),
        jax.ShapeDtypeStruct((nseq, POOL_TILE, POOL_WIDTH), F32),
        jax.ShapeDtypeStruct((nseq, CONV_TILE, SSD_CONV_DIM), F32),
        jax.ShapeDtypeStruct((DEPTH, nseq, SSD_WIDTH, SSD_STATE), F32),
        jax.ShapeDtypeStruct((nseq, CONV_TILE, LRU_WIDTH), F32),
        jax.ShapeDtypeStruct((nseq, SUBLANES, LRU_WIDTH), F32),
    ]
    h_scratch = (1, SSD_GROUPS, SSD_STATE, SSD_GWIDTH) if NC > 1 else (1, 1, SUBLANES, LANES)
    scratch = [
        pltpu.VMEM((S, POOL_TILE + T, POOL_WIDTH), F32),
        pltpu.VMEM((S, CONV_TILE, SSD_CONV_DIM), F32),
        pltpu.VMEM((S, CONV_TILE, LRU_WIDTH), F32),
        pltpu.VMEM((S, SUBLANES, LRU_WIDTH), F32),
        pltpu.VMEM(h_scratch, F32),
    ]
    if fused:
        scratch += [pltpu.VMEM((R, N_PROJ), F32), pltpu.VMEM((R, N_PROJ), F32), pltpu.VMEM((R, D_MIX), BF16)]
    return pl.pallas_call(
        functools.partial(_mix_kernel, S=S, T=T, NC=NC, LV=LV, pos0=pos0, fused=fused),
        grid=(nseq // S, NC),
        in_specs=in_specs,
        out_specs=out_specs,
        out_shape=out_shape,
        scratch_shapes=scratch,
        input_output_aliases=aliases,
        name="mix_seq%d" % S,
        compiler_params=_cparams(("arbitrary" if fused else "parallel", "arbitrary")),
    )(*lead_args, st[0], st[1], ssd_state, st[2], st[3], *w, ssd_new)


def _expand_matrix():
    e = np.zeros((LANES, SSD_WIDTH), np.float32)
    for k in range(SSD_HEADS):
        e[k, k * SSD_HEAD_DIM:(k + 1) * SSD_HEAD_DIM] = 1.0
    return jnp.asarray(np.concatenate([e, e], axis=0), BF16)


def _pair_block_diag(w):
    pairs = w.reshape(LRU_HEADS // 2, 2, LRU_HDIM, LRU_HDIM)
    eye = jnp.eye(2, dtype=w.dtype)
    return jnp.einsum('phij,hg->phigj', pairs, eye).reshape(LRU_HEADS // 2, 2 * LRU_HDIM, 2 * LRU_HDIM)


def _pad_lanes(v):
    return jnp.pad(v, (0, LANES - v.shape[0])).reshape(1, LANES)


def _matmul_weights(P):
    named = dict(w_in=P['w_in'], w_out=P['w_out'], wq=P['w_mem_q'], wo=P['w_mem_o'], wg=P['w_ffn_gate'],
                 wu=P['w_ffn_up'], wd=P['w_ffn_down'])
    big = {k: v.astype(BF16) for k, v in named.items()}
    w_in = big['w_in']
    off_dt = 3072
    big['w_in'] = jnp.concatenate(
        [w_in[:, :, :off_dt], w_in[:, :, off_dt + SSD_HEADS:], w_in[:, :, off_dt:off_dt + SSD_HEADS],
         jnp.zeros((DEPTH, D_MODEL, N_PROJ - C_DT - SSD_HEADS), BF16)], axis=2)
    return big


def _layer_params(l, P, big):
    mix_w = (
        P['pool_w'][l].astype(BF16),
        P['pool_scale'][l].reshape(1, POOL_WIDTH),
        P['ssd_conv_w'][l],
        P['ssd_conv_b'][l].reshape(1, SSD_CONV_DIM),
        _pad_lanes(P['ssd_dt_bias'][l]),
        _pad_lanes(P['ssd_a_log'][l]),
        jnp.repeat(P['ssd_d'][l], SSD_HEAD_DIM).reshape(1, SSD_WIDTH),
        P['ssd_norm'][l].reshape(1, SSD_WIDTH),
        P['lru_conv_w'][l],
        P['lru_conv_b'][l].reshape(1, LRU_WIDTH),
        jnp.concatenate([_pair_block_diag(P['lru_wa'][l]), _pair_block_diag(P['lru_wx'][l])], axis=2).astype(BF16),
        jnp.concatenate([P['lru_ba'][l].reshape(1, LRU_WIDTH), P['lru_bx'][l].reshape(1, LRU_WIDTH)], axis=1),
        P['lru_lambda'][l].reshape(1, LRU_WIDTH),
        _expand_matrix(),
    )
    out = dict(mix=mix_w, norm_mix=P['norm_mix'][l], norm_mem=P['norm_mem'][l], norm_ffn=P['norm_ffn'][l])
    out.update({k: (v, l) for k, v in big.items()})
    return out


def _run_trunk(h, layers, norm_final, states, ssd_state, attend, *, mix_cfg, fuse_proj):
    new_states = []
    ssd_new = None
    for l, lp in enumerate(layers):
        ssd_layer = min(l, ssd_state.shape[0] - 1)
        if fuse_proj:
            h, *st = _mix(h, states[l], ssd_state, ssd_layer, ssd_new, l, lp['mix'],
                          proj=(lp['norm_mix'], lp['w_in'], lp['w_out']), **mix_cfg)
        else:
            proj = _linear(h, lp['w_in'], name="in_proj", gain=lp['norm_mix'])
            ycat, *st = _mix(proj, states[l], ssd_state, ssd_layer, ssd_new, l, lp['mix'], **mix_cfg)
            h = _linear(ycat, lp['w_out'], name="out_proj", residual=h)
        ssd_new = st[2]
        h = attend(h, lp, l)
        last = l == len(layers) - 1
        h = _ffn(h, lp['norm_ffn'], lp['wg'], lp['wu'], lp['wd'], final_gain=norm_final if last else None)
        new_states.append(st)
    return h, new_states


def _unpack_states(new_states, nseq):
    pool = jnp.stack([s[0][:, 1:POOL_TILE] for s in new_states])
    sconv = jnp.stack([s[1][:, CONV_TILE - 3:CONV_TILE] for s in new_states])
    ssd = new_states[-1][2].reshape(DEPTH, nseq, SSD_HEADS, SSD_HEAD_DIM, SSD_STATE)
    lconv = jnp.stack([s[3][:, CONV_TILE - 3:CONV_TILE] for s in new_states])
    lru = jnp.stack([s[4][:, 0] for s in new_states])
    return pool, sconv, ssd, lconv, lru


def _trunk_prompt(x_prompt, mem_prompt, P, layers, *, chunk=256, tq=1024):
    batch, seq, _ = x_prompt.shape
    chunk = min(chunk, seq)
    tq = min(tq, seq)
    wk = P['w_mem_k'].astype(BF16)
    wv = P['w_mem_v'].astype(BF16)
    mem_k, mem_v = _kv_proj(mem_prompt.reshape(batch * N_MEM, D_MODEL), wk, wv)
    mem_k = mem_k.reshape(DEPTH, batch, N_MEM, D_MODEL)
    mem_v = mem_v.reshape(DEPTH, batch, N_MEM, D_MODEL)
    zeros = (jnp.zeros((batch, POOL_TILE, POOL_WIDTH), F32), jnp.zeros((batch, CONV_TILE, SSD_CONV_DIM), F32),
             jnp.zeros((batch, CONV_TILE, LRU_WIDTH), F32), jnp.zeros((batch, SUBLANES, LRU_WIDTH), F32))
    ssd0 = jnp.zeros((1, batch, SSD_WIDTH, SSD_STATE), F32)
    y, new_states = _run_trunk(
        x_prompt.reshape(batch * seq, D_MODEL), layers, P['norm_final'], [zeros] * DEPTH, ssd0,
        lambda h, lp, l: _attn_block(h, lp['norm_mem'], lp['wq'], lp['wo'], mem_k, mem_v, l, tq=tq, seq_rows=seq),
        mix_cfg=dict(S=1, T=chunk, NC=seq // chunk, LV=chunk, pos0=0), fuse_proj=True)
    return (y.reshape(batch, seq, D_MODEL),) + _unpack_states(new_states, batch) + (
        mem_k.reshape(DEPTH, batch, N_MEM, MEM_HEADS, MEM_HDIM),
        mem_v.reshape(DEPTH, batch, N_MEM, MEM_HEADS, MEM_HDIM))


def _trunk_sample(x_sample, past_len, state_pool, state_ssd_conv, state_ssd, state_lru_conv, state_lru,
                  cache_k, cache_v, P, layers, *, seq_blk=16, attn_blk=8):
    nseq, seq, _ = x_sample.shape
    assert seq <= SUBLANES
    pad_t = SUBLANES - seq
    h = jnp.pad(x_sample, ((0, 0), (0, pad_t), (0, 0))).reshape(nseq * SUBLANES, D_MODEL)
    states = []
    for l in range(DEPTH):
        states.append((
            jnp.pad(state_pool[l], ((0, 0), (POOL_TILE - POOL_BUF, 0), (0, 0))),
            jnp.pad(state_ssd_conv[l], ((0, 0), (CONV_TILE - 3, 0), (0, 0))),
            jnp.pad(state_lru_conv[l], ((0, 0), (CONV_TILE - 3, 0), (0, 0))),
            jnp.broadcast_to(state_lru[l][:, None, :], (nseq, SUBLANES, LRU_WIDTH)),
        ))
    kc, vc = _cache_view(cache_k), _cache_view(cache_v)

    def attend(h, lp, l):
        q = _linear(h, lp['wq'], name="q_proj", gain=lp['norm_mem'])
        return _linear(_cache_attn(q, kc, vc, l, nseq_blk=attn_blk), lp['wo'], name="o_proj", residual=h)

    y, new_states = _run_trunk(
        h, layers, P['norm_final'], states, state_ssd.reshape(DEPTH, nseq, SSD_WIDTH, SSD_STATE), attend,
        mix_cfg=dict(S=seq_blk, T=SUBLANES, NC=1, LV=seq, pos0=past_len), fuse_proj=False)
    y = y.reshape(nseq, SUBLANES, D_MODEL)[:, :seq]
    return (y,) + _unpack_states(new_states, nseq)


def kernel(x_prompt, x_sample, mem_prompt, state_pool, state_ssd_conv, state_ssd, state_lru_conv, state_lru,
           cache_mem_k, cache_mem_v, norm_mix, w_in, pool_w, pool_scale, ssd_conv_w, ssd_conv_b, ssd_dt_bias,
           ssd_a_log, ssd_d, ssd_norm, lru_conv_w, lru_conv_b, lru_wa, lru_ba, lru_wx, lru_bx, lru_lambda,
           w_out, norm_mem, w_mem_q, w_mem_k, w_mem_v, w_mem_o, norm_ffn, w_ffn_gate, w_ffn_up, w_ffn_down,
           norm_final):
    P = dict(norm_mix=norm_mix, w_in=w_in, pool_w=pool_w, pool_scale=pool_scale, ssd_conv_w=ssd_conv_w,
             ssd_conv_b=ssd_conv_b, ssd_dt_bias=ssd_dt_bias, ssd_a_log=ssd_a_log, ssd_d=ssd_d, ssd_norm=ssd_norm,
             lru_conv_w=lru_conv_w, lru_conv_b=lru_conv_b, lru_wa=lru_wa, lru_ba=lru_ba, lru_wx=lru_wx,
             lru_bx=lru_bx, lru_lambda=lru_lambda, w_out=w_out, norm_mem=norm_mem, w_mem_q=w_mem_q,
             w_mem_k=w_mem_k, w_mem_v=w_mem_v, w_mem_o=w_mem_o, norm_ffn=norm_ffn, w_ffn_gate=w_ffn_gate,
             w_ffn_up=w_ffn_up, w_ffn_down=w_ffn_down, norm_final=norm_final)
    big = _matmul_weights(P)
    layers = [_layer_params(l, P, big) for l in range(DEPTH)]
    (y_prompt, p_pool, p_sconv, p_ssd, p_lconv, p_lru, p_mem_k, p_mem_v) = _trunk_prompt(
        x_prompt, mem_prompt, P, layers)
    (y_sample, s_pool, s_sconv, s_ssd, s_lconv, s_lru) = _trunk_sample(
        x_sample, PAST_LEN, state_pool, state_ssd_conv, state_ssd, state_lru_conv, state_lru,
        cache_mem_k, cache_mem_v, P, layers)
    return (y_prompt, y_sample, p_pool, p_sconv, p_ssd, p_lconv, p_lru, p_mem_k, p_mem_v,
            s_pool, s_sconv, s_ssd, s_lconv, s_lru)
```

```python
import functools
import itertools
import math

import numpy as np
import jax
import jax.numpy as jnp
from jax import lax
from jax.experimental import pallas as pl
from jax.experimental.pallas import tpu as pltpu

F32 = jnp.float32
BF16 = jnp.bfloat16

D_MODEL = 1024
DEPTH = 4
EPS = 1e-6
POOL_WIDTH = 512
POOL_WINDOWS = (2, 4, 8, 16)
POOL_GDIM = 128
POOL_BUF = 15
SSD_WIDTH = 1024
SSD_HEAD_DIM = 64
SSD_HEADS = 16
SSD_GROUPS = 2
SSD_GWIDTH = SSD_WIDTH // SSD_GROUPS
SSD_STATE = 128
SSD_CONV_DIM = 1536
CONV_WIDTH = 4
LRU_WIDTH = 512
LRU_HEADS = 8
LRU_HDIM = 64
LRU_C = 8.0
N_MEM = 256
MEM_HEADS = 4
MEM_HDIM = 256
D_FF = 2816
D_MIX = 2048
PAST_LEN = 16384

C_POOL = 0
C_Z = 512
C_XBC = 1536
C_GATE = 3072
C_LRU = 3584
C_DT = 4096
N_PROJ = 4224

LANES = 128
SUBLANES = 8
SUBCHUNK = 128
CONV_TILE = SUBLANES
POOL_TILE = 2 * SUBLANES
VMEM_LIMIT = 56 * 1024 * 1024


def _cparams(sem):
    return pltpu.CompilerParams(dimension_semantics=sem, vmem_limit_bytes=VMEM_LIMIT)


def _resident(shape):
    nd = len(shape)
    return pl.BlockSpec(shape, lambda *_: (0,) * nd, pipeline_mode=pl.Buffered(1))


def _layer_resident(w):
    stacked, layer = w
    nd = stacked.ndim - 1
    return pl.BlockSpec((None,) + stacked.shape[1:], lambda *_: (layer,) + (0,) * nd,
                        pipeline_mode=pl.Buffered(1))


def _rms(x, g):
    var = jnp.mean(x * x, axis=-1, keepdims=True)
    return x * lax.rsqrt(var + EPS) * g


def _silu(x):
    return x * jax.nn.sigmoid(x)


def _softplus(x):
    return jnp.maximum(x, 0.0) + jnp.log1p(jnp.exp(-jnp.abs(x)))


def _gelu_tanh(x):
    c = math.sqrt(2.0 / math.pi)
    return x * (0.5 * (1.0 + jnp.tanh(c * (x + 0.044715 * (x * x * x)))))


def _split3(x):
    x1 = x.astype(BF16)
    r1 = x - x1.astype(F32)
    x2 = r1.astype(BF16)
    x3 = (r1 - x2.astype(F32)).astype(BF16)
    return x1, x2, x3


def _sel_dot(sel, x):
    x1, x2, x3 = _split3(x)
    d = functools.partial(jnp.dot, preferred_element_type=F32)
    return d(sel, x1) + d(sel, x2) + d(sel, x3)


def _spread(x, sel2):
    hi = x.astype(BF16)
    lo = (x - hi.astype(F32)).astype(BF16)
    return jnp.dot(jnp.concatenate([hi, lo], axis=1), sel2, preferred_element_type=F32)


def _col_chunks(n, step=512):
    return [(j, min(step, n - j)) for j in range(0, n, step)]


def _proj_weight(w_ref, tail_ref, j, n):
    if j + n <= C_GATE:
        return w_ref[:, j:j + n]
    assert j >= C_GATE
    return tail_ref[:, j - C_GATE:j - C_GATE + n]


def _linear_kernel(*refs, has_gain, has_res, has_tail=False):
    x_ref, w_ref = refs[0], refs[1]
    k = 2
    g_ref = r_ref = tail_ref = None
    if has_tail:
        tail_ref = refs[k]; k += 1
    if has_gain:
        g_ref = refs[k]; k += 1
    if has_res:
        r_ref = refs[k]; k += 1
    o_ref = refs[k]
    x = x_ref[...]
    if has_gain:
        x = _rms(x.astype(F32), g_ref[...])
    xb = x.astype(BF16)
    for j, n in _col_chunks(o_ref.shape[1]):
        w_cols = _proj_weight(w_ref, tail_ref, j, n) if has_tail else w_ref[:, j:j + n]
        y = jnp.dot(xb, w_cols, preferred_element_type=F32)
        if has_res:
            y = y + r_ref[:, j:j + n]
        o_ref[:, j:j + n] = y.astype(o_ref.dtype)


def _linear(x, w, *, name, tail=None, gain=None, residual=None, out_dtype=F32, tm=512):
    rows, kdim = x.shape
    n = w[0].shape[2] if tail is None else C_GATE + tail[0].shape[2]
    tm = min(tm, rows)
    assert rows % tm == 0
    in_specs = [pl.BlockSpec((tm, kdim), lambda i: (i, 0)), _layer_resident(w)]
    args = [x, w[0]]
    if tail is not None:
        in_specs.append(_layer_resident(tail))
        args.append(tail[0])
    if gain is not None:
        in_specs.append(_resident((1, kdim)))
        args.append(gain.reshape(1, kdim))
    if residual is not None:
        in_specs.append(pl.BlockSpec((tm, n), lambda i: (i, 0)))
        args.append(residual)
    return pl.pallas_call(
        functools.partial(_linear_kernel, has_gain=gain is not None, has_res=residual is not None,
                          has_tail=tail is not None),
        grid=(rows // tm,),
        in_specs=in_specs,
        out_specs=pl.BlockSpec((tm, n), lambda i: (i, 0)),
        out_shape=jax.ShapeDtypeStruct((rows, n), out_dtype),
        name=name,
        compiler_params=_cparams(("parallel",)),
    )(*args)


def _kv_kernel(x_ref, wk_ref, wv_ref, k_ref, v_ref):
    xb = x_ref[...].astype(BF16)
    k_ref[...] = jnp.dot(xb, wk_ref[...], preferred_element_type=F32)
    v_ref[...] = jnp.dot(xb, wv_ref[...], preferred_element_type=F32)


def _kv_proj(mem2d, wk, wv, tm=512):
    rows = mem2d.shape[0]
    tm = min(tm, rows)
    wspec = pl.BlockSpec((None, D_MODEL, D_MODEL), lambda l, i: (l, 0, 0))
    ospec = pl.BlockSpec((None, tm, D_MODEL), lambda l, i: (l, i, 0))
    oshape = jax.ShapeDtypeStruct((DEPTH, rows, D_MODEL), F32)
    return pl.pallas_call(
        _kv_kernel,
        grid=(DEPTH, rows // tm),
        in_specs=[pl.BlockSpec((tm, D_MODEL), lambda l, i: (i, 0)), wspec, wspec],
        out_specs=[ospec, ospec],
        out_shape=[oshape, oshape],
        name="kv_proj",
        compiler_params=_cparams(("parallel", "parallel")),
    )(mem2d, wk, wv)


def _ffn_kernel(x_ref, g_ref, wg_ref, wu_ref, wd_ref, *rest):
    o_ref = rest[-1]
    x = x_ref[...]
    ub = _rms(x, g_ref[...]).astype(BF16)
    acc = x
    for j, n in _col_chunks(D_FF, 1024):
        gate = jnp.dot(ub, wg_ref[:, j:j + n], preferred_element_type=F32)
        up = jnp.dot(ub, wu_ref[:, j:j + n], preferred_element_type=F32)
        act = (_silu(gate) * up).astype(BF16)
        acc = acc + jnp.dot(act, wd_ref[j:j + n, :], preferred_element_type=F32)
    if len(rest) == 2:
        acc = _rms(acc, rest[0][...])
    o_ref[...] = acc


def _ffn(h, gain, wg, wu, wd, final_gain=None, tm=1024):
    rows = h.shape[0]
    tm = min(tm, rows)
    row_spec = pl.BlockSpec((tm, D_MODEL), lambda i: (i, 0))
    in_specs = [row_spec, _resident((1, D_MODEL)), _layer_resident(wg), _layer_resident(wu), _layer_resident(wd)]
    args = [h, gain.reshape(1, D_MODEL), wg[0], wu[0], wd[0]]
    if final_gain is not None:
        in_specs.append(_resident((1, D_MODEL)))
        args.append(final_gain.reshape(1, D_MODEL))
    return pl.pallas_call(
        _ffn_kernel,
        grid=(rows // tm,),
        in_specs=in_specs,
        out_specs=row_spec,
        out_shape=jax.ShapeDtypeStruct((rows, D_MODEL), F32),
        name="ffn",
        compiler_params=_cparams(("parallel",)),
    )(*args)


def _attn_block_kernel(h_ref, g_ref, wq_ref, wo_ref, k_ref, v_ref, o_ref, kb_ref, vb_ref):
    scale = MEM_HDIM ** -0.5

    @pl.when(pl.program_id(1) == 0)
    def _cast_memory():
        kb_ref[...] = k_ref[0].astype(BF16)
        vb_ref[...] = v_ref[0].astype(BF16)

    x = h_ref[...]
    q = jnp.dot(_rms(x, g_ref[...]).astype(BF16), wq_ref[...], preferred_element_type=F32).astype(BF16)
    head_cols = [slice(hd * MEM_HDIM, (hd + 1) * MEM_HDIM) for hd in range(MEM_HEADS)]
    scores = [lax.dot_general(q[:, cols], kb_ref[:, cols], (((1,), (1,)), ((), ())),
                              preferred_element_type=F32) * scale for cols in head_cols]
    probs = []
    for sc in scores:
        e = jnp.exp(sc - jnp.max(sc, axis=-1, keepdims=True))
        probs.append((e / jnp.sum(e, axis=-1, keepdims=True)).astype(BF16))
    heads = [jnp.dot(p, vb_ref[:, cols], preferred_element_type=F32).astype(BF16)
             for p, cols in zip(probs, head_cols)]
    o = jnp.concatenate(heads, axis=1)
    o_ref[...] = x + jnp.dot(o, wo_ref[...], preferred_element_type=F32)


def _attn_block(h, gain, wq, wo, k, v, layer, *, tq, seq_rows):
    rows = h.shape[0]
    nseq = rows // seq_rows
    qblocks = seq_rows // tq
    h_spec = pl.BlockSpec((tq, D_MODEL), lambda i, c: (i * qblocks + c, 0))
    kv_spec = pl.BlockSpec((None, 1, N_MEM, D_MODEL), lambda i, c: (layer, i, 0, 0))
    return pl.pallas_call(
        _attn_block_kernel,
        grid=(nseq, qblocks),
        in_specs=[h_spec, _resident((1, D_MODEL)), _layer_resident(wq), _layer_resident(wo), kv_spec, kv_spec],
        out_specs=h_spec,
        out_shape=jax.ShapeDtypeStruct((rows, D_MODEL), F32),
        scratch_shapes=[pltpu.VMEM((N_MEM, D_MODEL), BF16), pltpu.VMEM((N_MEM, D_MODEL), BF16)],
        name="attn_block",
        compiler_params=_cparams(("parallel", "arbitrary")),
    )(h, gain.reshape(1, D_MODEL), wq[0], wo[0], k, v)


KV_SUB = 2 * MEM_HEADS
KV_ROWS = N_MEM * KV_SUB


def _cache_view(c):
    d, n = c.shape[:2]
    c = c.reshape(d, n, N_MEM, MEM_HEADS, MEM_HDIM // LANES, LANES)
    return jnp.transpose(c, (0, 1, 2, 4, 3, 5)).reshape(d, n, KV_ROWS, LANES)


def _cache_attn_kernel(q_ref, k_ref, v_ref, o_ref, *, nseq):
    scale = MEM_HDIM ** -0.5
    nvreg = KV_ROWS // LANES
    col_j = lax.broadcasted_iota(jnp.int32, (1, KV_ROWS), 1) % KV_SUB
    real = col_j < MEM_HEADS
    ts = []
    for s in range(nseq):
        q = q_ref[s * SUBLANES:(s + 1) * SUBLANES, :]
        pieces = []
        for j in range(KV_SUB):
            blk = (j % MEM_HEADS) * 2 + j // MEM_HEADS
            pieces.append(q[:, blk * LANES:(blk + 1) * LANES])
        qm = jnp.concatenate(pieces, axis=0).astype(BF16)
        kb = k_ref[s].astype(BF16)
        sc = lax.dot_general(qm, kb, (((1,), (1,)), ((), ())), preferred_element_type=F32)
        t = sc[0:SUBLANES]
        for j in range(1, KV_SUB):
            t = jnp.where(col_j == j, sc[j * SUBLANES:(j + 1) * SUBLANES], t)
        ts.append(t)
    t = jnp.concatenate(ts, axis=0)
    t = (t + pltpu.roll(t, KV_ROWS - MEM_HEADS, 1)) * scale
    t = jnp.where(real, t, jnp.finfo(F32).min)
    m = t[:, 0:LANES]
    for i in range(1, nvreg):
        m = jnp.maximum(m, t[:, i * LANES:(i + 1) * LANES])
    d = KV_SUB
    while d < LANES:
        m = jnp.maximum(m, pltpu.roll(m, d, 1))
        d *= 2
    e = jnp.where(real, jnp.exp(t - jnp.concatenate([m] * nvreg, axis=1)), 0.0)
    z = e[:, 0:LANES]
    for i in range(1, nvreg):
        z = z + e[:, i * LANES:(i + 1) * LANES]
    d = KV_SUB
    while d < LANES:
        z = z + pltpu.roll(z, d, 1)
        d *= 2
    z = jnp.where(real[:, 0:LANES], z, 1.0)
    p = e / jnp.concatenate([z] * nvreg, axis=1)
    p = p + pltpu.roll(p, MEM_HEADS, 1)
    for s in range(nseq):
        p_s = p[s * SUBLANES:(s + 1) * SUBLANES]
        pm = jnp.concatenate([jnp.where(col_j == j, p_s, 0.0) for j in range(KV_SUB)], axis=0).astype(BF16)
        o = jnp.dot(pm, v_ref[s].astype(BF16), preferred_element_type=F32)
        for blk in range(KV_SUB):
            j = (blk % 2) * MEM_HEADS + blk // 2
            o_ref[s * SUBLANES:(s + 1) * SUBLANES, blk * LANES:(blk + 1) * LANES] = (
                o[j * SUBLANES:(j + 1) * SUBLANES, :].astype(o_ref.dtype))


def _cache_attn(q, k, v, layer, *, nseq_blk=4):
    rows = q.shape[0]
    nseq = rows // SUBLANES
    assert q.shape[1] == KV_SUB * LANES and nseq % nseq_blk == 0
    q_spec = pl.BlockSpec((nseq_blk * SUBLANES, D_MODEL), lambda i: (i, 0))
    kv_spec = pl.BlockSpec((None, nseq_blk, KV_ROWS, LANES), lambda i: (layer, i, 0, 0))
    return pl.pallas_call(
        functools.partial(_cache_attn_kernel, nseq=nseq_blk),
        grid=(nseq // nseq_blk,),
        in_specs=[q_spec, kv_spec, kv_spec],
        out_specs=q_spec,
        out_shape=jax.ShapeDtypeStruct((rows, D_MODEL), BF16),
        name="cache_attn",
        compiler_params=_cparams(("parallel",)),
    )(q, k, v)


def _causal_conv(xt, carry, w_ref, b_ref, cols, *, T):
    n = xt.shape[-1]
    sub = lax.broadcasted_iota(jnp.int32, (1, SUBLANES, 1), 1)
    y = b_ref[:, cols].reshape(1, 1, n) + w_ref[CONV_WIDTH - 1:CONV_WIDTH, cols].reshape(1, 1, n) * xt
    for d in range(1, CONV_WIDTH):
        cur = pltpu.roll(xt, d, 1)
        before = pltpu.roll(carry, d, 1)
        if T != SUBLANES:
            before = jnp.concatenate([before, cur[:-1]], axis=0)
        tap = w_ref[CONV_WIDTH - 1 - d:CONV_WIDTH - d, cols].reshape(1, 1, n)
        y = y + tap * jnp.where(sub < d, before, cur)
    return y


def _conv_carry(xt, *, T, LV):
    if T != SUBLANES:
        assert LV == T
        return xt[-1:]
    assert LV >= CONV_WIDTH - 1
    return xt if LV == SUBLANES else pltpu.roll(xt, SUBLANES - LV, 1)


def _mix_chunk(x, ycat_ref, c, refs, *, S, T, NC, LV, pos0, side_work=(), late_work=()):
    pending = [iter(side_work)]

    def tick():
        task = next(pending[0], None)
        if task is not None:
            task()

    (pool_st, sconv_st, ssd_st, lconv_st, lru_st,
     pool_w, pool_scale, sconv_w, sconv_b, dt_bias, a_log, d_skip, ssd_norm,
     lconv_w, lconv_b, gate_w, gate_b, lam, expand, ssd_prev,
     out_ref, pool_out, sconv_out, ssd_out, lconv_out, lru_out,
     ext_pool, ext_sconv, ext_lconv, lru_carry, h_state) = refs
    R = S * T
    carried = NC > 1
    row = lax.broadcasted_iota(jnp.int32, (R, 1), 0)
    t_in = row % T
    pos = pos0 + c * LV + t_in

    u_pool = x[:, C_POOL:C_POOL + POOL_WIDTH]
    ext_pool[:, POOL_TILE:, :] = u_pool.reshape(S, T, POOL_WIDTH)
    for g, w in enumerate(POOL_WINDOWS):
        cols = slice(g * POOL_GDIM, (g + 1) * POOL_GDIM)
        win = ext_pool[:, :, cols].reshape(S * (POOL_TILE + T), POOL_GDIM)
        shift = 1
        while shift < w:
            win = win + pltpu.roll(win, shift, 0)
            shift *= 2
        acc = win.reshape(S, POOL_TILE + T, POOL_GDIM)[:, POOL_TILE:, :]
        cnt = jnp.minimum(pos + 1, w).astype(F32)
        pooled = acc.reshape(R, POOL_GDIM) / cnt - u_pool[:, cols]
        y = jnp.dot(pooled.astype(BF16), pool_w[g], preferred_element_type=F32)
        ycat_ref[:, cols] = (y * pool_scale[:, cols]).astype(ycat_ref.dtype)
    new_pool = ext_pool[:, pl.ds(LV, POOL_TILE), :]
    ext_pool[:, 0:POOL_TILE, :] = new_pool

    conv_blocks = []
    for j, n in _col_chunks(SSD_CONV_DIM):
        cols = slice(j, j + n)
        xt = x[:, C_XBC + j:C_XBC + j + n].reshape(R // SUBLANES, SUBLANES, n)
        conv = _causal_conv(xt, ext_sconv[:, :, cols], sconv_w, sconv_b, cols, T=T)
        ext_sconv[:, :, cols] = _conv_carry(xt, T=T, LV=LV)
        conv_blocks.append(_silu(conv).reshape(R, n))
    xbc = jnp.concatenate(conv_blocks, axis=1)

    lane = lax.broadcasted_iota(jnp.int32, (1, LANES), 1)
    dt = _softplus(x[:, C_DT:C_DT + LANES] + dt_bias[...])
    dt = jnp.where((lane < SSD_HEADS) & (t_in < LV), dt, 0.0)
    d_a = dt * (-jnp.exp(a_log[...]))
    z = x[:, C_Z:C_Z + SSD_WIDTH]

    li = lax.broadcasted_iota(jnp.int32, (SUBCHUNK, SUBCHUNK), 0)
    si = lax.broadcasted_iota(jnp.int32, (SUBCHUNK, SUBCHUNK), 1)
    same_seq = (li // T) == (si // T) if S > 1 else (li >= 0)
    causal = same_seq & (si <= li)
    causal_b = causal.astype(BF16)
    same_b = same_seq.astype(BF16)
    lane_q = lax.broadcasted_iota(jnp.int32, (SUBCHUNK, LANES), 1)
    row_q = lax.broadcasted_iota(jnp.int32, (SUBCHUNK, 1), 0)
    expand_m = expand[...]

    for q in range(R // SUBCHUNK):
        rs = slice(q * SUBCHUNK, (q + 1) * SUBCHUNK)
        xs = xbc[rs, 0:SSD_WIDTH]
        d_a_q = d_a[rs]
        acs = _sel_dot(causal_b, d_a_q)
        tot = _sel_dot(same_b, d_a_q)
        acs_row = acs.T
        dt_e = _spread(dt[rs], expand_m)
        eacs_e = _spread(jnp.exp(acs), expand_m)
        etot = jnp.exp(tot)
        etot_e = _spread(etot[0:SUBLANES] if carried else etot, expand_m)
        xdt = xs * dt_e
        xdt_b = xdt.astype(BF16)
        xdecay = xdt * _spread(jnp.exp(tot - acs), expand_m)
        xdecay_b = xdecay.astype(BF16)
        tick()
        y_parts = []
        for g in range(SSD_GROUPS):
            gc = slice(g * SSD_GWIDTH, (g + 1) * SSD_GWIDTH)
            b_g = xbc[rs, SSD_WIDTH + g * SSD_STATE:SSD_WIDTH + (g + 1) * SSD_STATE].astype(BF16)
            c_g = xbc[rs, SSD_WIDTH + (SSD_GROUPS + g) * SSD_STATE:
                      SSD_WIDTH + (SSD_GROUPS + g + 1) * SSD_STATE].astype(BF16)
            cb = lax.dot_general(c_g, b_g, (((1,), (1,)), ((), ())), preferred_element_type=F32)
            diag = []
            for j in range(SSD_GWIDTH // LANES):
                pair = []
                for hh in range(2):
                    k = g * (SSD_HEADS // SSD_GROUPS) + 2 * j + hh
                    seg = acs[:, k:k + 1] - acs_row[k:k + 1, :]
                    lmat = jnp.where(causal, jnp.exp(jnp.where(causal, seg, 0.0)), 0.0)
                    pair.append((cb * lmat).astype(BF16))
                xp = xdt_b[:, g * SSD_GWIDTH + j * LANES:g * SSD_GWIDTH + (j + 1) * LANES]
                top = jnp.where(lane_q < SSD_HEAD_DIM, xp, jnp.zeros_like(xp))
                bot = jnp.where(lane_q >= SSD_HEAD_DIM, xp, jnp.zeros_like(xp))
                diag.append(jnp.dot(jnp.concatenate(pair, axis=1), jnp.concatenate([top, bot], axis=0),
                                    preferred_element_type=F32))
            y_g = jnp.concatenate(diag, axis=1)
            if carried:
                h_t = h_state[0, g]
                y_g = y_g + jnp.dot(c_g, h_t.astype(BF16), preferred_element_type=F32) * eacs_e[:, gc]
                upd = lax.dot_general(b_g, xdecay_b[:, gc], (((0,), (0,)), ((), ())),
                                      preferred_element_type=F32)
                h_state[0, g] = h_t * etot_e[0:1, gc] + upd
            else:
                xd_t = xdecay[:, gc].T.astype(BF16)
                etot_t = etot_e[:, gc].T
                y_off = jnp.zeros((SUBCHUNK, SSD_GWIDTH), F32)
                for s in range(S):
                    mine = (row_q // T) == s
                    h_s = ssd_st[s, gc, :]
                    c_s = jnp.where(mine, c_g, jnp.zeros_like(c_g))
                    b_s = jnp.where(mine, b_g, jnp.zeros_like(b_g))
                    y_off = y_off + lax.dot_general(c_s, h_s.astype(BF16), (((1,), (1,)), ((), ())),
                                                    preferred_element_type=F32)
                    upd = jnp.dot(xd_t, b_s, preferred_element_type=F32)
                    ssd_out[s, gc, :] = h_s * etot_t[:, s * T:s * T + 1] + upd
                y_g = y_g + y_off * eacs_e[:, gc]
            y_parts.append(y_g)
            tick()
        y = jnp.concatenate(y_parts, axis=1) + xs * d_skip[...]
        y = _rms(y * _silu(z[rs]), ssd_norm[...])
        ycat_ref[rs, POOL_WIDTH:POOL_WIDTH + SSD_WIDTH] = y.astype(ycat_ref.dtype)

    pending[0] = itertools.chain(pending[0], late_work)
    xt = x[:, C_LRU:C_LRU + LRU_WIDTH].reshape(R // SUBLANES, SUBLANES, LRU_WIDTH)
    xc = _causal_conv(xt, ext_lconv[...], lconv_w, lconv_b, slice(0, LRU_WIDTH), T=T).reshape(R, LRU_WIDTH)
    ext_lconv[...] = _conv_carry(xt, T=T, LV=LV)
    tick()
    xc_b = xc.astype(BF16)
    r_lin, i_lin = [], []
    for p in range(LRU_WIDTH // LANES):
        g2 = jnp.dot(xc_b[:, p * LANES:(p + 1) * LANES], gate_w[p], preferred_element_type=F32)
        r_lin.append(g2[:, 0:LANES])
        i_lin.append(g2[:, LANES:2 * LANES])
    r_gate = jax.nn.sigmoid(jnp.concatenate(r_lin, axis=1) + gate_b[:, 0:LRU_WIDTH])
    i_gate = jax.nn.sigmoid(jnp.concatenate(i_lin, axis=1) + gate_b[:, LRU_WIDTH:2 * LRU_WIDTH])
    log_a = (-LRU_C) * r_gate * _softplus(-lam[...])
    a = jnp.exp(log_a)
    mult = jnp.where(pos == 0, 1.0, jnp.sqrt(1.0 - jnp.exp(2.0 * log_a)))
    b = mult * i_gate * xc
    tick()
    tiles = R // SUBLANES
    a3 = a.reshape(tiles, SUBLANES, LRU_WIDTH)
    b3 = b.reshape(tiles, SUBLANES, LRU_WIDTH)
    sub = lax.broadcasted_iota(jnp.int32, (1, SUBLANES, 1), 1)
    d = 1
    while d < SUBLANES:
        keep = sub >= d
        a_prev = jnp.where(keep, pltpu.roll(a3, d, 1), 1.0)
        b_prev = jnp.where(keep, pltpu.roll(b3, d, 1), 0.0)
        b3 = a3 * b_prev + b3
        a3 = a3 * a_prev
        d *= 2
        tick()
    if S == 1:
        h_prev = lru_carry[0, 0:1, :]
        h_tiles = []
        for i in range(tiles):
            h_i = a3[i] * h_prev + b3[i]
            h_tiles.append(h_i)
            h_prev = h_i[SUBLANES - 1:SUBLANES, :]
        h = jnp.concatenate(h_tiles, axis=0)
    else:
        h = (a3 * lru_carry[...] + b3).reshape(R, LRU_WIDTH)
    gate = x[:, C_GATE:C_GATE + LRU_WIDTH]
    ycat_ref[:, POOL_WIDTH + SSD_WIDTH:D_MIX] = (h * _gelu_tanh(gate)).astype(ycat_ref.dtype)
    if S == 1:
        lru_carry[0] = jnp.broadcast_to(h[LV - 1:LV, :], (SUBLANES, LRU_WIDTH))
    else:
        li_r = lax.broadcasted_iota(jnp.int32, (R, R), 0)
        si_r = lax.broadcasted_iota(jnp.int32, (R, R), 1)
        pick_last = ((li_r // T) == (si_r // T)) & ((si_r % T) == LV - 1)
        lru_carry[...] = _sel_dot(pick_last.astype(BF16), h).reshape(S, T, LRU_WIDTH)
    for task in pending[0]:
        task()


def _mix_kernel(*refs, S, T, NC, LV, pos0, fused):
    refs = list(refs)
    if fused:
        h_ref, h_next, norm_g, w_in, w_tail, w_out = refs[:6]
        del refs[:6]
    else:
        x = refs.pop(0)
    io = tuple(refs[:31])
    (pool_st, sconv_st, ssd_st, lconv_st, lru_st) = io[:5]
    (out_ref, pool_out, sconv_out, ssd_out, lconv_out, lru_out,
     ext_pool, ext_sconv, ext_lconv, lru_carry, h_state) = io[20:]
    cfg = dict(S=S, T=T, NC=NC, LV=LV, pos0=pos0)
    c = pl.program_id(1)
    carried = NC > 1

    @pl.when(c == 0)
    def _load_state():
        ext_pool[:, 0:POOL_TILE, :] = pool_st[...]
        ext_sconv[:, 0:CONV_TILE, :] = sconv_st[...]
        ext_lconv[:, 0:CONV_TILE, :] = lconv_st[...]
        lru_carry[...] = lru_st[...]
        if carried:
            for g in range(SSD_GROUPS):
                h_state[0, g] = ssd_st[0, g * SSD_GWIDTH:(g + 1) * SSD_GWIDTH, :].T

    if not fused:
        _mix_chunk(x, out_ref, c, io, **cfg)
    else:
        proj_even, proj_odd, ycat_ref = refs[31:34]

        def projection_steps(src, dst):
            normed = []

            def column_step(j, n):
                if not normed:
                    normed.append(_rms(src[...], norm_g[...]).astype(BF16))
                dst[:, j:j + n] = jnp.dot(normed[0], _proj_weight(w_in, w_tail, j, n),
                                          preferred_element_type=F32)

            return [functools.partial(column_step, j, n) for j, n in _col_chunks(N_PROJ)]

        def run_chunk(cur, side_work):
            early = POOL_WIDTH + SSD_WIDTH

            def project_early_columns():
                out_ref[...] = h_ref[...] + jnp.dot(ycat_ref[:, 0:early], w_out[0:early, :],
                                                    preferred_element_type=F32)

            _mix_chunk(cur, ycat_ref, c, io, side_work=side_work, late_work=[project_early_columns], **cfg)
            out_ref[...] += jnp.dot(ycat_ref[:, early:D_MIX], w_out[early:D_MIX, :],
                                    preferred_element_type=F32)

        @pl.when((c == 0) & (pl.program_id(0) == 0))
        def _first_projection():
            for step in projection_steps(h_ref, proj_even):
                step()

        for parity, cur, nxt in ((0, proj_even, proj_odd), (1, proj_odd, proj_even)):
            @pl.when(c % 2 == parity)
            def _chunk(cur=cur, nxt=nxt):
                run_chunk(cur, projection_steps(h_next, nxt))

    @pl.when(c == NC - 1)
    def _store_state():
        pool_out[...] = ext_pool[:, 0:POOL_TILE, :]
        sconv_out[...] = ext_sconv[:, 0:CONV_TILE, :]
        lconv_out[...] = ext_lconv[:, 0:CONV_TILE, :]
        lru_out[...] = lru_carry[...]
        if carried:
            for g in range(SSD_GROUPS):
                ssd_out[0, g * SSD_GWIDTH:(g + 1) * SSD_GWIDTH, :] = h_state[0, g].T


def _mix(x, st, ssd_state, ssd_layer, ssd_new, out_layer, w, *, S, T, NC, LV, pos0, proj=None):
    rows = x.shape[0]
    nseq = rows // (NC * T)
    R = S * T
    fused = proj is not None
    assert nseq % S == 0 and R % SUBCHUNK == 0
    assert (S == 1) or (NC == 1 and R == SUBCHUNK and T == SUBLANES)
    seq3 = lambda i, c: (i, 0, 0)
    row_blk = lambda i, c: (i * NC + c, 0)
    if fused:
        assert NC % 2 == 0
        next_blk = lambda i, c: (jnp.minimum(i * NC + c + 1, nseq * NC - 1), 0)
        lead_specs = [pl.BlockSpec((R, D_MODEL), row_blk), pl.BlockSpec((R, D_MODEL), next_blk),
                      _resident((1, D_MODEL))] + [_layer_resident(wl) for wl in proj[1:]]
        lead_args = [x, x, proj[0].reshape(1, D_MODEL)] + [wl[0] for wl in proj[1:]]
    else:
        lead_specs = [pl.BlockSpec((R, N_PROJ), row_blk)]
        lead_args = [x]
    in_specs = lead_specs + [
        pl.BlockSpec((S, POOL_TILE, POOL_WIDTH), seq3),
        pl.BlockSpec((S, CONV_TILE, SSD_CONV_DIM), seq3),
        pl.BlockSpec((None, S, SSD_WIDTH, SSD_STATE), lambda i, c: (ssd_layer, i, 0, 0)),
        pl.BlockSpec((S, CONV_TILE, LRU_WIDTH), seq3),
        pl.BlockSpec((S, SUBLANES, LRU_WIDTH), seq3),
    ] + [_resident(a.shape) for a in w] + [pl.BlockSpec(memory_space=pl.ANY)]
    if ssd_new is None:
        ssd_new = jnp.zeros((DEPTH, nseq, SSD_WIDTH, SSD_STATE), F32)
    aliases = {len(in_specs) - 1: 3}
    out_specs = [
        pl.BlockSpec((R, D_MODEL if fused else D_MIX), row_blk),
        pl.BlockSpec((S, POOL_TILE, POOL_WIDTH), seq3),
        pl.BlockSpec((S, CONV_TILE, SSD_CONV_DIM), seq3),
        pl.BlockSpec((None, S, SSD_WIDTH, SSD_STATE), lambda i, c: (out_layer, i, 0, 0)),
        pl.BlockSpec((S, CONV_TILE, LRU_WIDTH), seq3),
        pl.BlockSpec((S, SUBLANES, LRU_WIDTH), seq3),
    ]
    out_shape = [
        jax.ShapeDtypeStruct((rows, D_MODEL), F32) if fused else jax.ShapeDtypeStruct((rows, D_MIX), BF16),
        jax.ShapeDtypeStruct((nseq, POOL_TILE, POOL_WIDTH), F32),
        jax.ShapeDtypeStruct((nseq, CONV_TILE, SSD_CONV_DIM), F32),
        jax.ShapeDtypeStruct((DEPTH, nseq, SSD_WIDTH, SSD_STATE), F32),
        jax.ShapeDtypeStruct((nseq, CONV_TILE, LRU_WIDTH), F32),
        jax.ShapeDtypeStruct((nseq, SUBLANES, LRU_WIDTH), F32),
    ]
    h_scratch = (1, SSD_GROUPS, SSD_STATE, SSD_GWIDTH) if NC > 1 else (1, 1, SUBLANES, LANES)
    scratch = [
        pltpu.VMEM((S, POOL_TILE + T, POOL_WIDTH), F32),
        pltpu.VMEM((S, CONV_TILE, SSD_CONV_DIM), F32),
        pltpu.VMEM((S, CONV_TILE, LRU_WIDTH), F32),
        pltpu.VMEM((S, SUBLANES, LRU_WIDTH), F32),
        pltpu.VMEM(h_scratch, F32),
    ]
    if fused:
        scratch += [pltpu.VMEM((R, N_PROJ), F32), pltpu.VMEM((R, N_PROJ), F32), pltpu.VMEM((R, D_MIX), BF16)]
    return pl.pallas_call(
        functools.partial(_mix_kernel, S=S, T=T, NC=NC, LV=LV, pos0=pos0, fused=fused),
        grid=(nseq // S, NC),
        in_specs=in_specs,
        out_specs=out_specs,
        out_shape=out_shape,
        scratch_shapes=scratch,
        input_output_aliases=aliases,
        name="mix_seq%d" % S,
        compiler_params=_cparams(("arbitrary" if fused else "parallel", "arbitrary")),
    )(*lead_args, st[0], st[1], ssd_state, st[2], st[3], *w, ssd_new)


def _expand_matrix():
    e = np.zeros((LANES, SSD_WIDTH), np.float32)
    for k in range(SSD_HEADS):
        e[k, k * SSD_HEAD_DIM:(k + 1) * SSD_HEAD_DIM] = 1.0
    return jnp.asarray(np.concatenate([e, e], axis=0), BF16)


def _pair_block_diag(w):
    pairs = w.reshape(LRU_HEADS // 2, 2, LRU_HDIM, LRU_HDIM)
    eye = jnp.eye(2, dtype=w.dtype)
    return jnp.einsum('phij,hg->phigj', pairs, eye).reshape(LRU_HEADS // 2, 2 * LRU_HDIM, 2 * LRU_HDIM)


def _pad_lanes(v):
    return jnp.pad(v, (0, LANES - v.shape[0])).reshape(1, LANES)


def _matmul_weights(P):
    named = dict(w_in=P['w_in'], w_out=P['w_out'], wq=P['w_mem_q'], wo=P['w_mem_o'], wg=P['w_ffn_gate'],
                 wu=P['w_ffn_up'], wd=P['w_ffn_down'])
    big = {k: v.astype(BF16) for k, v in named.items()}
    w_in = big['w_in']
    big['w_in_tail'] = jnp.concatenate(
        [w_in[:, :, C_GATE + SSD_HEADS:], w_in[:, :, C_GATE:C_GATE + SSD_HEADS],
         jnp.zeros((DEPTH, D_MODEL, N_PROJ - C_DT - SSD_HEADS), BF16)], axis=2)
    return big


def _layer_params(l, P, big):
    mix_w = (
        P['pool_w'][l].astype(BF16),
        P['pool_scale'][l].reshape(1, POOL_WIDTH),
        P['ssd_conv_w'][l],
        P['ssd_conv_b'][l].reshape(1, SSD_CONV_DIM),
        _pad_lanes(P['ssd_dt_bias'][l]),
        _pad_lanes(P['ssd_a_log'][l]),
        jnp.repeat(P['ssd_d'][l], SSD_HEAD_DIM).reshape(1, SSD_WIDTH),
        P['ssd_norm'][l].reshape(1, SSD_WIDTH),
        P['lru_conv_w'][l],
        P['lru_conv_b'][l].reshape(1, LRU_WIDTH),
        jnp.concatenate([_pair_block_diag(P['lru_wa'][l]), _pair_block_diag(P['lru_wx'][l])], axis=2).astype(BF16),
        jnp.concatenate([P['lru_ba'][l].reshape(1, LRU_WIDTH), P['lru_bx'][l].reshape(1, LRU_WIDTH)], axis=1),
        P['lru_lambda'][l].reshape(1, LRU_WIDTH),
        _expand_matrix(),
    )
    out = dict(mix=mix_w, norm_mix=P['norm_mix'][l], norm_mem=P['norm_mem'][l], norm_ffn=P['norm_ffn'][l])
    out.update({k: (v, l) for k, v in big.items()})
    return out


def _run_trunk(h, layers, norm_final, states, ssd_state, attend, *, mix_cfg, fuse_proj):
    new_states = []
    ssd_new = None
    for l, lp in enumerate(layers):
        ssd_layer = min(l, ssd_state.shape[0] - 1)
        if fuse_proj:
            h, *st = _mix(h, states[l], ssd_state, ssd_layer, ssd_new, l, lp['mix'],
                          proj=(lp['norm_mix'], lp['w_in'], lp['w_in_tail'], lp['w_out']), **mix_cfg)
        else:
            proj = _linear(h, lp['w_in'], name="in_proj", tail=lp['w_in_tail'], gain=lp['norm_mix'])
            ycat, *st = _mix(proj, states[l], ssd_state, ssd_layer, ssd_new, l, lp['mix'], **mix_cfg)
            h = _linear(ycat, lp['w_out'], name="out_proj", residual=h)
        ssd_new = st[2]
        h = attend(h, lp, l)
        last = l == len(layers) - 1
        h = _ffn(h, lp['norm_ffn'], lp['wg'], lp['wu'], lp['wd'], final_gain=norm_final if last else None)
        new_states.append(st)
    return h, new_states


def _unpack_states(new_states, nseq):
    pool = jnp.stack([s[0][:, 1:POOL_TILE] for s in new_states])
    sconv = jnp.stack([s[1][:, CONV_TILE - 3:CONV_TILE] for s in new_states])
    ssd = new_states[-1][2].reshape(DEPTH, nseq, SSD_HEADS, SSD_HEAD_DIM, SSD_STATE)
    lconv = jnp.stack([s[3][:, CONV_TILE - 3:CONV_TILE] for s in new_states])
    lru = jnp.stack([s[4][:, 0] for s in new_states])
    return pool, sconv, ssd, lconv, lru


def _trunk_prompt(x_prompt, mem_prompt, P, layers, *, chunk=256, tq=1024):
    batch, seq, _ = x_prompt.shape
    chunk = min(chunk, seq)
    tq = min(tq, seq)
    wk = P['w_mem_k'].astype(BF16)
    wv = P['w_mem_v'].astype(BF16)
    mem_k, mem_v = _kv_proj(mem_prompt.reshape(batch * N_MEM, D_MODEL), wk, wv)
    mem_k = mem_k.reshape(DEPTH, batch, N_MEM, D_MODEL)
    mem_v = mem_v.reshape(DEPTH, batch, N_MEM, D_MODEL)
    zeros = (jnp.zeros((batch, POOL_TILE, POOL_WIDTH), F32), jnp.zeros((batch, CONV_TILE, SSD_CONV_DIM), F32),
             jnp.zeros((batch, CONV_TILE, LRU_WIDTH), F32), jnp.zeros((batch, SUBLANES, LRU_WIDTH), F32))
    ssd0 = jnp.zeros((1, batch, SSD_WIDTH, SSD_STATE), F32)
    y, new_states = _run_trunk(
        x_prompt.reshape(batch * seq, D_MODEL), layers, P['norm_final'], [zeros] * DEPTH, ssd0,
        lambda h, lp, l: _attn_block(h, lp['norm_mem'], lp['wq'], lp['wo'], mem_k, mem_v, l, tq=tq, seq_rows=seq),
        mix_cfg=dict(S=1, T=chunk, NC=seq // chunk, LV=chunk, pos0=0), fuse_proj=True)
    return (y.reshape(batch, seq, D_MODEL),) + _unpack_states(new_states, batch) + (
        mem_k.reshape(DEPTH, batch, N_MEM, MEM_HEADS, MEM_HDIM),
        mem_v.reshape(DEPTH, batch, N_MEM, MEM_HEADS, MEM_HDIM))


def _trunk_sample(x_sample, past_len, state_pool, state_ssd_conv, state_ssd, state_lru_conv, state_lru,
                  cache_k, cache_v, P, layers, *, seq_blk=16, attn_blk=8):
    nseq, seq, _ = x_sample.shape
    assert seq <= SUBLANES
    pad_t = SUBLANES - seq
    h = jnp.pad(x_sample, ((0, 0), (0, pad_t), (0, 0))).reshape(nseq * SUBLANES, D_MODEL)
    states = []
    for l in range(DEPTH):
        states.append((
            jnp.pad(state_pool[l], ((0, 0), (POOL_TILE - POOL_BUF, 0), (0, 0))),
            jnp.pad(state_ssd_conv[l], ((0, 0), (CONV_TILE - 3, 0), (0, 0))),
            jnp.pad(state_lru_conv[l], ((0, 0), (CONV_TILE - 3, 0), (0, 0))),
            jnp.broadcast_to(state_lru[l][:, None, :], (nseq, SUBLANES, LRU_WIDTH)),
        ))
    kc, vc = _cache_view(cache_k), _cache_view(cache_v)

    def attend(h, lp, l):
        q = _linear(h, lp['wq'], name="q_proj", gain=lp['norm_mem'])
        return _linear(_cache_attn(q, kc, vc, l, nseq_blk=attn_blk), lp['wo'], name="o_proj", residual=h)

    y, new_states = _run_trunk(
        h, layers, P['norm_final'], states, state_ssd.reshape(DEPTH, nseq, SSD_WIDTH, SSD_STATE), attend,
        mix_cfg=dict(S=seq_blk, T=SUBLANES, NC=1, LV=seq, pos0=past_len), fuse_proj=False)
    y = y.reshape(nseq, SUBLANES, D_MODEL)[:, :seq]
    return (y,) + _unpack_states(new_states, nseq)


def kernel(x_prompt, x_sample, mem_prompt, state_pool, state_ssd_conv, state_ssd, state_lru_conv, state_lru,
           cache_mem_k, cache_mem_v, norm_mix, w_in, pool_w, pool_scale, ssd_conv_w, ssd_conv_b, ssd_dt_bias,
           ssd_a_log, ssd_d, ssd_norm, lru_conv_w, lru_conv_b, lru_wa, lru_ba, lru_wx, lru_bx, lru_lambda,
           w_out, norm_mem, w_mem_q, w_mem_k, w_mem_v, w_mem_o, norm_ffn, w_ffn_gate, w_ffn_up, w_ffn_down,
           norm_final):
    P = dict(norm_mix=norm_mix, w_in=w_in, pool_w=pool_w, pool_scale=pool_scale, ssd_conv_w=ssd_conv_w,
             ssd_conv_b=ssd_conv_b, ssd_dt_bias=ssd_dt_bias, ssd_a_log=ssd_a_log, ssd_d=ssd_d, ssd_norm=ssd_norm,
             lru_conv_w=lru_conv_w, lru_conv_b=lru_conv_b, lru_wa=lru_wa, lru_ba=lru_ba, lru_wx=lru_wx,
             lru_bx=lru_bx, lru_lambda=lru_lambda, w_out=w_out, norm_mem=norm_mem, w_mem_q=w_mem_q,
             w_mem_k=w_mem_k, w_mem_v=w_mem_v, w_mem_o=w_mem_o, norm_ffn=norm_ffn, w_ffn_gate=w_ffn_gate,
             w_ffn_up=w_ffn_up, w_ffn_down=w_ffn_down, norm_final=norm_final)
    big = _matmul_weights(P)
    layers = [_layer_params(l, P, big) for l in range(DEPTH)]
    (y_prompt, p_pool, p_sconv, p_ssd, p_lconv, p_lru, p_mem_k, p_mem_v) = _trunk_prompt(
        x_prompt, mem_prompt, P, layers)
    (y_sample, s_pool, s_sconv, s_ssd, s_lconv, s_lru) = _trunk_sample(
        x_sample, PAST_LEN, state_pool, state_ssd_conv, state_ssd, state_lru_conv, state_lru,
        cache_mem_k, cache_mem_v, P, layers)
    return (y_prompt, y_sample, p_pool, p_sconv, p_ssd, p_lconv, p_lru, p_mem_k, p_mem_v,
            s_pool, s_sconv, s_ssd, s_lconv, s_lru)
```

```python
import functools
import itertools
import math

import numpy as np
import jax
import jax.numpy as jnp
from jax import lax
from jax.experimental import pallas as pl
from jax.experimental.pallas import tpu as pltpu

F32 = jnp.float32
BF16 = jnp.bfloat16

D_MODEL = 1024
DEPTH = 4
EPS = 1e-6
POOL_WIDTH = 512
POOL_WINDOWS = (2, 4, 8, 16)
POOL_GDIM = 128
POOL_BUF = 15
SSD_WIDTH = 1024
SSD_HEAD_DIM = 64
SSD_HEADS = 16
SSD_GROUPS = 2
SSD_GWIDTH = SSD_WIDTH // SSD_GROUPS
SSD_STATE = 128
SSD_CONV_DIM = 1536
CONV_WIDTH = 4
LRU_WIDTH = 512
LRU_HEADS = 8
LRU_HDIM = 64
LRU_C = 8.0
N_MEM = 256
MEM_HEADS = 4
MEM_HDIM = 256
D_FF = 2816
D_MIX = 2048
PAST_LEN = 16384

C_POOL = 0
C_Z = 512
C_XBC = 1536
C_GATE = 3072
C_LRU = 3584
C_DT = 4096
N_PROJ = 4224

LANES = 128
SUBLANES = 8
SUBCHUNK = 128
CONV_TILE = SUBLANES
POOL_TILE = 2 * SUBLANES
VMEM_LIMIT = 56 * 1024 * 1024


def _cparams(sem):
    return pltpu.CompilerParams(dimension_semantics=sem, vmem_limit_bytes=VMEM_LIMIT)


def _resident(shape):
    nd = len(shape)
    return pl.BlockSpec(shape, lambda *_: (0,) * nd, pipeline_mode=pl.Buffered(1))


def _layer_resident(w):
    stacked, layer = w
    nd = stacked.ndim - 1
    return pl.BlockSpec((None,) + stacked.shape[1:], lambda *_: (layer,) + (0,) * nd,
                        pipeline_mode=pl.Buffered(1))


def _rms(x, g):
    var = jnp.mean(x * x, axis=-1, keepdims=True)
    return x * lax.rsqrt(var + EPS) * g


def _silu(x):
    return x * jax.nn.sigmoid(x)


def _softplus(x):
    return jnp.maximum(x, 0.0) + jnp.log1p(jnp.exp(-jnp.abs(x)))


def _gelu_tanh(x):
    c = math.sqrt(2.0 / math.pi)
    return x * (0.5 * (1.0 + jnp.tanh(c * (x + 0.044715 * (x * x * x)))))


def _split3(x):
    x1 = x.astype(BF16)
    r1 = x - x1.astype(F32)
    x2 = r1.astype(BF16)
    x3 = (r1 - x2.astype(F32)).astype(BF16)
    return x1, x2, x3


def _sel_dot(sel, x):
    x1, x2, x3 = _split3(x)
    d = functools.partial(jnp.dot, preferred_element_type=F32)
    return d(sel, x1) + d(sel, x2) + d(sel, x3)


def _spread(x, sel2):
    hi = x.astype(BF16)
    lo = (x - hi.astype(F32)).astype(BF16)
    return jnp.dot(jnp.concatenate([hi, lo], axis=1), sel2, preferred_element_type=F32)


def _col_chunks(n, step=512):
    return [(j, min(step, n - j)) for j in range(0, n, step)]


def _proj_weight(w_ref, tail_ref, j, n):
    if j + n <= C_GATE:
        return w_ref[:, j:j + n]
    assert j >= C_GATE
    return tail_ref[:, j - C_GATE:j - C_GATE + n]


def _linear_kernel(*refs, has_gain, has_res, has_tail=False):
    x_ref, w_ref = refs[0], refs[1]
    k = 2
    g_ref = r_ref = tail_ref = None
    if has_tail:
        tail_ref = refs[k]; k += 1
    if has_gain:
        g_ref = refs[k]; k += 1
    if has_res:
        r_ref = refs[k]; k += 1
    o_ref = refs[k]
    x = x_ref[...]
    if has_gain:
        x = _rms(x.astype(F32), g_ref[...])
    xb = x.astype(BF16)
    for j, n in _col_chunks(o_ref.shape[1]):
        w_cols = _proj_weight(w_ref, tail_ref, j, n) if has_tail else w_ref[:, j:j + n]
        y = jnp.dot(xb, w_cols, preferred_element_type=F32)
        if has_res:
            y = y + r_ref[:, j:j + n]
        o_ref[:, j:j + n] = y.astype(o_ref.dtype)


def _linear(x, w, *, name, tail=None, gain=None, residual=None, out_dtype=F32, tm=512):
    rows, kdim = x.shape
    n = w[0].shape[2] if tail is None else C_GATE + tail[0].shape[2]
    tm = min(tm, rows)
    assert rows % tm == 0
    in_specs = [pl.BlockSpec((tm, kdim), lambda i: (i, 0)), _layer_resident(w)]
    args = [x, w[0]]
    if tail is not None:
        in_specs.append(_layer_resident(tail))
        args.append(tail[0])
    if gain is not None:
        in_specs.append(_resident((1, kdim)))
        args.append(gain.reshape(1, kdim))
    if residual is not None:
        in_specs.append(pl.BlockSpec((tm, n), lambda i: (i, 0)))
        args.append(residual)
    return pl.pallas_call(
        functools.partial(_linear_kernel, has_gain=gain is not None, has_res=residual is not None,
                          has_tail=tail is not None),
        grid=(rows // tm,),
        in_specs=in_specs,
        out_specs=pl.BlockSpec((tm, n), lambda i: (i, 0)),
        out_shape=jax.ShapeDtypeStruct((rows, n), out_dtype),
        name=name,
        compiler_params=_cparams(("parallel",)),
    )(*args)


def _kv_kernel(x_ref, wk_ref, wv_ref, k_ref, v_ref):
    xb = x_ref[...].astype(BF16)
    k_ref[...] = jnp.dot(xb, wk_ref[...], preferred_element_type=F32)
    v_ref[...] = jnp.dot(xb, wv_ref[...], preferred_element_type=F32)


def _kv_proj(mem2d, wk, wv, tm=512):
    rows = mem2d.shape[0]
    tm = min(tm, rows)
    wspec = pl.BlockSpec((None, D_MODEL, D_MODEL), lambda l, i: (l, 0, 0))
    ospec = pl.BlockSpec((None, tm, D_MODEL), lambda l, i: (l, i, 0))
    oshape = jax.ShapeDtypeStruct((DEPTH, rows, D_MODEL), F32)
    return pl.pallas_call(
        _kv_kernel,
        grid=(DEPTH, rows // tm),
        in_specs=[pl.BlockSpec((tm, D_MODEL), lambda l, i: (i, 0)), wspec, wspec],
        out_specs=[ospec, ospec],
        out_shape=[oshape, oshape],
        name="kv_proj",
        compiler_params=_cparams(("parallel", "parallel")),
    )(mem2d, wk, wv)


def _ffn_kernel(x_ref, g_ref, wg_ref, wu_ref, wd_ref, *rest):
    o_ref = rest[-1]
    x = x_ref[...]
    ub = _rms(x, g_ref[...]).astype(BF16)
    acc = x
    for j, n in _col_chunks(D_FF, 1024):
        gate = jnp.dot(ub, wg_ref[:, j:j + n], preferred_element_type=F32)
        up = jnp.dot(ub, wu_ref[:, j:j + n], preferred_element_type=F32)
        act = (_silu(gate) * up).astype(BF16)
        acc = acc + jnp.dot(act, wd_ref[j:j + n, :], preferred_element_type=F32)
    if len(rest) == 2:
        acc = _rms(acc, rest[0][...])
    o_ref[...] = acc


def _ffn(h, gain, wg, wu, wd, final_gain=None, tm=1024):
    rows = h.shape[0]
    tm = min(tm, rows)
    row_spec = pl.BlockSpec((tm, D_MODEL), lambda i: (i, 0))
    in_specs = [row_spec, _resident((1, D_MODEL)), _layer_resident(wg), _layer_resident(wu), _layer_resident(wd)]
    args = [h, gain.reshape(1, D_MODEL), wg[0], wu[0], wd[0]]
    if final_gain is not None:
        in_specs.append(_resident((1, D_MODEL)))
        args.append(final_gain.reshape(1, D_MODEL))
    return pl.pallas_call(
        _ffn_kernel,
        grid=(rows // tm,),
        in_specs=in_specs,
        out_specs=row_spec,
        out_shape=jax.ShapeDtypeStruct((rows, D_MODEL), F32),
        name="ffn",
        compiler_params=_cparams(("parallel",)),
    )(*args)


def _attn_block_kernel(h_ref, g_ref, wq_ref, wo_ref, k_ref, v_ref, o_ref, kb_ref, vb_ref):
    scale = MEM_HDIM ** -0.5

    @pl.when(pl.program_id(1) == 0)
    def _cast_memory():
        kb_ref[...] = k_ref[0].astype(BF16)
        vb_ref[...] = v_ref[0].astype(BF16)

    x = h_ref[...]
    q = jnp.dot(_rms(x, g_ref[...]).astype(BF16), wq_ref[...], preferred_element_type=F32).astype(BF16)
    head_cols = [slice(hd * MEM_HDIM, (hd + 1) * MEM_HDIM) for hd in range(MEM_HEADS)]
    scores = [lax.dot_general(q[:, cols], kb_ref[:, cols], (((1,), (1,)), ((), ())),
                              preferred_element_type=F32) * scale for cols in head_cols]
    probs = []
    for sc in scores:
        e = jnp.exp(sc - jnp.max(sc, axis=-1, keepdims=True))
        probs.append((e / jnp.sum(e, axis=-1, keepdims=True)).astype(BF16))
    heads = [jnp.dot(p, vb_ref[:, cols], preferred_element_type=F32).astype(BF16)
             for p, cols in zip(probs, head_cols)]
    o = jnp.concatenate(heads, axis=1)
    o_ref[...] = x + jnp.dot(o, wo_ref[...], preferred_element_type=F32)


def _attn_block(h, gain, wq, wo, k, v, layer, *, tq, seq_rows):
    rows = h.shape[0]
    nseq = rows // seq_rows
    qblocks = seq_rows // tq
    h_spec = pl.BlockSpec((tq, D_MODEL), lambda i, c: (i * qblocks + c, 0))
    kv_spec = pl.BlockSpec((None, 1, N_MEM, D_MODEL), lambda i, c: (layer, i, 0, 0))
    return pl.pallas_call(
        _attn_block_kernel,
        grid=(nseq, qblocks),
        in_specs=[h_spec, _resident((1, D_MODEL)), _layer_resident(wq), _layer_resident(wo), kv_spec, kv_spec],
        out_specs=h_spec,
        out_shape=jax.ShapeDtypeStruct((rows, D_MODEL), F32),
        scratch_shapes=[pltpu.VMEM((N_MEM, D_MODEL), BF16), pltpu.VMEM((N_MEM, D_MODEL), BF16)],
        name="attn_block",
        compiler_params=_cparams(("parallel", "arbitrary")),
    )(h, gain.reshape(1, D_MODEL), wq[0], wo[0], k, v)


KV_SUB = 2 * MEM_HEADS
KV_ROWS = N_MEM * KV_SUB


def _cache_view(c):
    d, n = c.shape[:2]
    c = c.reshape(d, n, N_MEM, MEM_HEADS, MEM_HDIM // LANES, LANES)
    return jnp.transpose(c, (0, 1, 2, 4, 3, 5)).reshape(d, n, KV_ROWS, LANES)


def _cache_attn_kernel(q_ref, k_ref, v_ref, o_ref, *, nseq):
    scale = MEM_HDIM ** -0.5
    nvreg = KV_ROWS // LANES
    col_j = lax.broadcasted_iota(jnp.int32, (1, KV_ROWS), 1) % KV_SUB
    real = col_j < MEM_HEADS
    ts = []
    for s in range(nseq):
        q = q_ref[s * SUBLANES:(s + 1) * SUBLANES, :]
        pieces = []
        for j in range(KV_SUB):
            blk = (j % MEM_HEADS) * 2 + j // MEM_HEADS
            pieces.append(q[:, blk * LANES:(blk + 1) * LANES])
        qm = jnp.concatenate(pieces, axis=0).astype(BF16)
        kb = k_ref[s].astype(BF16)
        sc = lax.dot_general(qm, kb, (((1,), (1,)), ((), ())), preferred_element_type=F32)
        t = sc[0:SUBLANES]
        for j in range(1, KV_SUB):
            t = jnp.where(col_j == j, sc[j * SUBLANES:(j + 1) * SUBLANES], t)
        ts.append(t)
    t = jnp.concatenate(ts, axis=0)
    t = (t + pltpu.roll(t, KV_ROWS - MEM_HEADS, 1)) * scale
    t = jnp.where(real, t, jnp.finfo(F32).min)
    m = t[:, 0:LANES]
    for i in range(1, nvreg):
        m = jnp.maximum(m, t[:, i * LANES:(i + 1) * LANES])
    d = KV_SUB
    while d < LANES:
        m = jnp.maximum(m, pltpu.roll(m, d, 1))
        d *= 2
    e = jnp.where(real, jnp.exp(t - jnp.concatenate([m] * nvreg, axis=1)), 0.0)
    z = e[:, 0:LANES]
    for i in range(1, nvreg):
        z = z + e[:, i * LANES:(i + 1) * LANES]
    d = KV_SUB
    while d < LANES:
        z = z + pltpu.roll(z, d, 1)
        d *= 2
    z = jnp.where(real[:, 0:LANES], z, 1.0)
    p = e / jnp.concatenate([z] * nvreg, axis=1)
    p = p + pltpu.roll(p, MEM_HEADS, 1)
    for s in range(nseq):
        p_s = p[s * SUBLANES:(s + 1) * SUBLANES]
        pm = jnp.concatenate([jnp.where(col_j == j, p_s, 0.0) for j in range(KV_SUB)], axis=0).astype(BF16)
        o = jnp.dot(pm, v_ref[s].astype(BF16), preferred_element_type=F32)
        for blk in range(KV_SUB):
            j = (blk % 2) * MEM_HEADS + blk // 2
            o_ref[s * SUBLANES:(s + 1) * SUBLANES, blk * LANES:(blk + 1) * LANES] = (
                o[j * SUBLANES:(j + 1) * SUBLANES, :].astype(o_ref.dtype))


def _cache_attn(q, k, v, layer, *, nseq_blk=4):
    rows = q.shape[0]
    nseq = rows // SUBLANES
    assert q.shape[1] == KV_SUB * LANES and nseq % nseq_blk == 0
    q_spec = pl.BlockSpec((nseq_blk * SUBLANES, D_MODEL), lambda i: (i, 0))
    kv_spec = pl.BlockSpec((None, nseq_blk, KV_ROWS, LANES), lambda i: (layer, i, 0, 0))
    return pl.pallas_call(
        functools.partial(_cache_attn_kernel, nseq=nseq_blk),
        grid=(nseq // nseq_blk,),
        in_specs=[q_spec, kv_spec, kv_spec],
        out_specs=q_spec,
        out_shape=jax.ShapeDtypeStruct((rows, D_MODEL), BF16),
        name="cache_attn",
        compiler_params=_cparams(("parallel",)),
    )(q, k, v)


def _causal_conv(xt, carry, w_ref, b_ref, cols, *, T):
    n = xt.shape[-1]
    sub = lax.broadcasted_iota(jnp.int32, (1, SUBLANES, 1), 1)
    y = b_ref[:, cols].reshape(1, 1, n) + w_ref[CONV_WIDTH - 1:CONV_WIDTH, cols].reshape(1, 1, n) * xt
    for d in range(1, CONV_WIDTH):
        cur = pltpu.roll(xt, d, 1)
        before = pltpu.roll(carry, d, 1)
        if T != SUBLANES:
            before = jnp.concatenate([before, cur[:-1]], axis=0)
        tap = w_ref[CONV_WIDTH - 1 - d:CONV_WIDTH - d, cols].reshape(1, 1, n)
        y = y + tap * jnp.where(sub < d, before, cur)
    return y


def _conv_carry(xt, *, T, LV):
    if T != SUBLANES:
        assert LV == T
        return xt[-1:]
    assert LV >= CONV_WIDTH - 1
    return xt if LV == SUBLANES else pltpu.roll(xt, SUBLANES - LV, 1)


def _mix_chunk(x, ycat_ref, c, refs, *, S, T, NC, LV, pos0, side_work=(), late_work=()):
    pending = [iter(side_work)]

    def tick():
        task = next(pending[0], None)
        if task is not None:
            task()

    (pool_st, sconv_st, ssd_st, lconv_st, lru_st,
     pool_w, pool_scale, sconv_w, sconv_b, dt_bias, a_log, d_skip, ssd_norm,
     lconv_w, lconv_b, gate_w, gate_b, lam, expand, ssd_prev,
     out_ref, pool_out, sconv_out, ssd_out, lconv_out, lru_out,
     ext_pool, ext_sconv, ext_lconv, lru_carry, h_state) = refs
    R = S * T
    carried = NC > 1
    row = lax.broadcasted_iota(jnp.int32, (R, 1), 0)
    t_in = row % T
    pos = pos0 + c * LV + t_in

    u_pool = x[:, C_POOL:C_POOL + POOL_WIDTH]
    ext_pool[:, POOL_TILE:, :] = u_pool.reshape(S, T, POOL_WIDTH)
    for g, w in enumerate(POOL_WINDOWS):
        cols = slice(g * POOL_GDIM, (g + 1) * POOL_GDIM)
        win = ext_pool[:, :, cols].reshape(S * (POOL_TILE + T), POOL_GDIM)
        shift = 1
        while shift < w:
            win = win + pltpu.roll(win, shift, 0)
            shift *= 2
        acc = win.reshape(S, POOL_TILE + T, POOL_GDIM)[:, POOL_TILE:, :]
        cnt = jnp.minimum(pos + 1, w).astype(F32)
        pooled = acc.reshape(R, POOL_GDIM) / cnt - x[:, C_POOL + g * POOL_GDIM:C_POOL + (g + 1) * POOL_GDIM]
        y = jnp.dot(pooled.astype(BF16), pool_w[g], preferred_element_type=F32)
        ycat_ref[:, cols] = (y * pool_scale[:, cols]).astype(ycat_ref.dtype)
    new_pool = ext_pool[:, pl.ds(LV, POOL_TILE), :]
    ext_pool[:, 0:POOL_TILE, :] = new_pool

    conv_blocks = []
    for j, n in _col_chunks(SSD_CONV_DIM):
        cols = slice(j, j + n)
        xt = x[:, C_XBC + j:C_XBC + j + n].reshape(R // SUBLANES, SUBLANES, n)
        conv = _causal_conv(xt, ext_sconv[:, :, cols], sconv_w, sconv_b, cols, T=T)
        ext_sconv[:, :, cols] = _conv_carry(xt, T=T, LV=LV)
        conv_blocks.append(_silu(conv).reshape(R, n))
    xbc = jnp.concatenate(conv_blocks, axis=1)

    lane = lax.broadcasted_iota(jnp.int32, (1, LANES), 1)
    dt = _softplus(x[:, C_DT:C_DT + LANES] + dt_bias[...])
    dt = jnp.where((lane < SSD_HEADS) & (t_in < LV), dt, 0.0)
    d_a = dt * (-jnp.exp(a_log[...]))

    li = lax.broadcasted_iota(jnp.int32, (SUBCHUNK, SUBCHUNK), 0)
    si = lax.broadcasted_iota(jnp.int32, (SUBCHUNK, SUBCHUNK), 1)
    same_seq = (li // T) == (si // T) if S > 1 else (li >= 0)
    causal = same_seq & (si <= li)
    causal_b = causal.astype(BF16)
    same_b = same_seq.astype(BF16)
    lane_q = lax.broadcasted_iota(jnp.int32, (SUBCHUNK, LANES), 1)
    row_q = lax.broadcasted_iota(jnp.int32, (SUBCHUNK, 1), 0)
    expand_m = expand[...]

    for q in range(R // SUBCHUNK):
        rs = slice(q * SUBCHUNK, (q + 1) * SUBCHUNK)
        xs = xbc[rs, 0:SSD_WIDTH]
        d_a_q = d_a[rs]
        acs = _sel_dot(causal_b, d_a_q)
        tot = _sel_dot(same_b, d_a_q)
        acs_row = acs.T
        dt_e = _spread(dt[rs], expand_m)
        eacs_e = _spread(jnp.exp(acs), expand_m)
        etot = jnp.exp(tot)
        etot_e = _spread(etot[0:SUBLANES] if carried else etot, expand_m)
        xdt = xs * dt_e
        xdt_b = xdt.astype(BF16)
        xdecay = xdt * _spread(jnp.exp(tot - acs), expand_m)
        xdecay_b = xdecay.astype(BF16)
        tick()
        y_parts = []
        for g in range(SSD_GROUPS):
            gc = slice(g * SSD_GWIDTH, (g + 1) * SSD_GWIDTH)
            b_g = xbc[rs, SSD_WIDTH + g * SSD_STATE:SSD_WIDTH + (g + 1) * SSD_STATE].astype(BF16)
            c_g = xbc[rs, SSD_WIDTH + (SSD_GROUPS + g) * SSD_STATE:
                      SSD_WIDTH + (SSD_GROUPS + g + 1) * SSD_STATE].astype(BF16)
            cb = lax.dot_general(c_g, b_g, (((1,), (1,)), ((), ())), preferred_element_type=F32)
            diag = []
            for j in range(SSD_GWIDTH // LANES):
                pair = []
                for hh in range(2):
                    k = g * (SSD_HEADS // SSD_GROUPS) + 2 * j + hh
                    seg = acs[:, k:k + 1] - acs_row[k:k + 1, :]
                    lmat = jnp.where(causal, jnp.exp(jnp.where(causal, seg, 0.0)), 0.0)
                    pair.append((cb * lmat).astype(BF16))
                xp = xdt_b[:, g * SSD_GWIDTH + j * LANES:g * SSD_GWIDTH + (j + 1) * LANES]
                top = jnp.where(lane_q < SSD_HEAD_DIM, xp, jnp.zeros_like(xp))
                bot = jnp.where(lane_q >= SSD_HEAD_DIM, xp, jnp.zeros_like(xp))
                diag.append(jnp.dot(jnp.concatenate(pair, axis=1), jnp.concatenate([top, bot], axis=0),
                                    preferred_element_type=F32))
            y_g = jnp.concatenate(diag, axis=1)
            if carried:
                h_t = h_state[0, g]
                y_g = y_g + jnp.dot(c_g, h_t.astype(BF16), preferred_element_type=F32) * eacs_e[:, gc]
                upd = lax.dot_general(b_g, xdecay_b[:, gc], (((0,), (0,)), ((), ())),
                                      preferred_element_type=F32)
                h_state[0, g] = h_t * etot_e[0:1, gc] + upd
            else:
                xd_t = xdecay[:, gc].T.astype(BF16)
                etot_t = etot_e[:, gc].T
                y_off = jnp.zeros((SUBCHUNK, SSD_GWIDTH), F32)
                for s in range(S):
                    mine = (row_q // T) == s
                    h_s = ssd_st[s, gc, :]
                    c_s = jnp.where(mine, c_g, jnp.zeros_like(c_g))
                    b_s = jnp.where(mine, b_g, jnp.zeros_like(b_g))
                    y_off = y_off + lax.dot_general(c_s, h_s.astype(BF16), (((1,), (1,)), ((), ())),
                                                    preferred_element_type=F32)
                    upd = jnp.dot(xd_t, b_s, preferred_element_type=F32)
                    ssd_out[s, gc, :] = h_s * etot_t[:, s * T:s * T + 1] + upd
                y_g = y_g + y_off * eacs_e[:, gc]
            y_parts.append(y_g)
            tick()
        y = jnp.concatenate(y_parts, axis=1) + xs * d_skip[...]
        y = _rms(y * _silu(x[rs, C_Z:C_Z + SSD_WIDTH]), ssd_norm[...])
        ycat_ref[rs, POOL_WIDTH:POOL_WIDTH + SSD_WIDTH] = y.astype(ycat_ref.dtype)

    pending[0] = itertools.chain(pending[0], late_work)
    xt = x[:, C_LRU:C_LRU + LRU_WIDTH].reshape(R // SUBLANES, SUBLANES, LRU_WIDTH)
    xc = _causal_conv(xt, ext_lconv[...], lconv_w, lconv_b, slice(0, LRU_WIDTH), T=T).reshape(R, LRU_WIDTH)
    ext_lconv[...] = _conv_carry(xt, T=T, LV=LV)
    tick()
    xc_b = xc.astype(BF16)
    r_lin, i_lin = [], []
    for p in range(LRU_WIDTH // LANES):
        g2 = jnp.dot(xc_b[:, p * LANES:(p + 1) * LANES], gate_w[p], preferred_element_type=F32)
        r_lin.append(g2[:, 0:LANES])
        i_lin.append(g2[:, LANES:2 * LANES])
    r_gate = jax.nn.sigmoid(jnp.concatenate(r_lin, axis=1) + gate_b[:, 0:LRU_WIDTH])
    i_gate = jax.nn.sigmoid(jnp.concatenate(i_lin, axis=1) + gate_b[:, LRU_WIDTH:2 * LRU_WIDTH])
    log_a = (-LRU_C) * r_gate * _softplus(-lam[...])
    a = jnp.exp(log_a)
    mult = jnp.where(pos == 0, 1.0, jnp.sqrt(1.0 - jnp.exp(2.0 * log_a)))
    b = mult * i_gate * xc
    tick()
    tiles = R // SUBLANES
    a3 = a.reshape(tiles, SUBLANES, LRU_WIDTH)
    b3 = b.reshape(tiles, SUBLANES, LRU_WIDTH)
    sub = lax.broadcasted_iota(jnp.int32, (1, SUBLANES, 1), 1)
    d = 1
    while d < SUBLANES:
        keep = sub >= d
        a_prev = jnp.where(keep, pltpu.roll(a3, d, 1), 1.0)
        b_prev = jnp.where(keep, pltpu.roll(b3, d, 1), 0.0)
        b3 = a3 * b_prev + b3
        a3 = a3 * a_prev
        d *= 2
        tick()
    if S == 1:
        h_prev = lru_carry[0, 0:1, :]
        h_tiles = []
        for i in range(tiles):
            h_i = a3[i] * h_prev + b3[i]
            h_tiles.append(h_i)
            h_prev = h_i[SUBLANES - 1:SUBLANES, :]
        h = jnp.concatenate(h_tiles, axis=0)
    else:
        h = (a3 * lru_carry[...] + b3).reshape(R, LRU_WIDTH)
    gate = x[:, C_GATE:C_GATE + LRU_WIDTH]
    ycat_ref[:, POOL_WIDTH + SSD_WIDTH:D_MIX] = (h * _gelu_tanh(gate)).astype(ycat_ref.dtype)
    if S == 1:
        lru_carry[0] = jnp.broadcast_to(h[LV - 1:LV, :], (SUBLANES, LRU_WIDTH))
    else:
        li_r = lax.broadcasted_iota(jnp.int32, (R, R), 0)
        si_r = lax.broadcasted_iota(jnp.int32, (R, R), 1)
        pick_last = ((li_r // T) == (si_r // T)) & ((si_r % T) == LV - 1)
        lru_carry[...] = _sel_dot(pick_last.astype(BF16), h).reshape(S, T, LRU_WIDTH)
    for task in pending[0]:
        task()


def _mix_kernel(*refs, S, T, NC, LV, pos0, fused):
    refs = list(refs)
    if fused:
        h_ref, h_next, norm_g, w_in, w_tail, w_out = refs[:6]
        del refs[:6]
    else:
        x = refs.pop(0)
    io = tuple(refs[:31])
    (pool_st, sconv_st, ssd_st, lconv_st, lru_st) = io[:5]
    (out_ref, pool_out, sconv_out, ssd_out, lconv_out, lru_out,
     ext_pool, ext_sconv, ext_lconv, lru_carry, h_state) = io[20:]
    cfg = dict(S=S, T=T, NC=NC, LV=LV, pos0=pos0)
    c = pl.program_id(1)
    carried = NC > 1

    @pl.when(c == 0)
    def _load_state():
        ext_pool[:, 0:POOL_TILE, :] = pool_st[...]
        ext_sconv[:, 0:CONV_TILE, :] = sconv_st[...]
        ext_lconv[:, 0:CONV_TILE, :] = lconv_st[...]
        lru_carry[...] = lru_st[...]
        if carried:
            for g in range(SSD_GROUPS):
                h_state[0, g] = ssd_st[0, g * SSD_GWIDTH:(g + 1) * SSD_GWIDTH, :].T

    if not fused:
        _mix_chunk(x, out_ref, c, io, **cfg)
    else:
        proj_even, proj_odd, ycat_ref = refs[31:34]

        def projection_steps(src, dst):
            normed = []

            def column_step(j, n):
                if not normed:
                    normed.append(_rms(src[...], norm_g[...]).astype(BF16))
                dst[:, j:j + n] = jnp.dot(normed[0], _proj_weight(w_in, w_tail, j, n),
                                          preferred_element_type=F32)

            return [functools.partial(column_step, j, n) for j, n in _col_chunks(N_PROJ)]

        def run_chunk(cur, side_work):
            early = POOL_WIDTH + SSD_WIDTH

            def project_early_columns():
                out_ref[...] = h_ref[...] + jnp.dot(ycat_ref[:, 0:early], w_out[0:early, :],
                                                    preferred_element_type=F32)

            _mix_chunk(cur, ycat_ref, c, io, side_work=side_work, late_work=[project_early_columns], **cfg)
            out_ref[...] += jnp.dot(ycat_ref[:, early:D_MIX], w_out[early:D_MIX, :],
                                    preferred_element_type=F32)

        @pl.when((c == 0) & (pl.program_id(0) == 0))
        def _first_projection():
            for step in projection_steps(h_ref, proj_even):
                step()

        for parity, cur, nxt in ((0, proj_even, proj_odd), (1, proj_odd, proj_even)):
            @pl.when(c % 2 == parity)
            def _chunk(cur=cur, nxt=nxt):
                run_chunk(cur, projection_steps(h_next, nxt))

    @pl.when(c == NC - 1)
    def _store_state():
        pool_out[...] = ext_pool[:, 0:POOL_TILE, :]
        sconv_out[...] = ext_sconv[:, 0:CONV_TILE, :]
        lconv_out[...] = ext_lconv[:, 0:CONV_TILE, :]
        lru_out[...] = lru_carry[...]
        if carried:
            for g in range(SSD_GROUPS):
                ssd_out[0, g * SSD_GWIDTH:(g + 1) * SSD_GWIDTH, :] = h_state[0, g].T


def _mix(x, st, ssd_state, ssd_layer, ssd_new, out_layer, w, *, S, T, NC, LV, pos0, proj=None):
    rows = x.shape[0]
    nseq = rows // (NC * T)
    R = S * T
    fused = proj is not None
    assert nseq % S == 0 and R % SUBCHUNK == 0
    assert (S == 1) or (NC == 1 and R == SUBCHUNK and T == SUBLANES)
    seq3 = lambda i, c: (i, 0, 0)
    row_blk = lambda i, c: (i * NC + c, 0)
    if fused:
        assert NC % 2 == 0
        next_blk = lambda i, c: (jnp.minimum(i * NC + c + 1, nseq * NC - 1), 0)
        lead_specs = [pl.BlockSpec((R, D_MODEL), row_blk), pl.BlockSpec((R, D_MODEL), next_blk),
                      _resident((1, D_MODEL))] + [_layer_resident(wl) for wl in proj[1:]]
        lead_args = [x, x, proj[0].reshape(1, D_MODEL)] + [wl[0] for wl in proj[1:]]
    else:
        lead_specs = [pl.BlockSpec((R, N_PROJ), row_blk)]
        lead_args = [x]
    in_specs = lead_specs + [
        pl.BlockSpec((S, POOL_TILE, POOL_WIDTH), seq3),
        pl.BlockSpec((S, CONV_TILE, SSD_CONV_DIM), seq3),
        pl.BlockSpec((None, S, SSD_WIDTH, SSD_STATE), lambda i, c: (ssd_layer, i, 0, 0)),
        pl.BlockSpec((S, CONV_TILE, LRU_WIDTH), seq3),
        pl.BlockSpec((S, SUBLANES, LRU_WIDTH), seq3),
    ] + [_resident(a.shape) for a in w] + [pl.BlockSpec(memory_space=pl.ANY)]
    if ssd_new is None:
        ssd_new = jnp.zeros((DEPTH, nseq, SSD_WIDTH, SSD_STATE), F32)
    aliases = {len(in_specs) - 1: 3}
    out_specs = [
        pl.BlockSpec((R, D_MODEL if fused else D_MIX), row_blk),
        pl.BlockSpec((S, POOL_TILE, POOL_WIDTH), seq3),
        pl.BlockSpec((S, CONV_TILE, SSD_CONV_DIM), seq3),
        pl.BlockSpec((None, S, SSD_WIDTH, SSD_STATE), lambda i, c: (out_layer, i, 0, 0)),
        pl.BlockSpec((S, CONV_TILE, LRU_WIDTH), seq3),
        pl.BlockSpec((S, SUBLANES, LRU_WIDTH), seq3),
    ]
    out_shape = [
        jax.ShapeDtypeStruct((rows, D_MODEL), F32) if fused else jax.ShapeDtypeStruct((rows, D_MIX), BF16),
        jax.ShapeDtypeStruct((nseq, POOL_TILE, POOL_WIDTH), F32),
        jax.ShapeDtypeStruct((nseq, CONV_TILE, SSD_CONV_DIM), F32),
        jax.ShapeDtypeStruct((DEPTH, nseq, SSD_WIDTH, SSD_STATE), F32),
        jax.ShapeDtypeStruct((nseq, CONV_TILE, LRU_WIDTH), F32),
        jax.ShapeDtypeStruct((nseq, SUBLANES, LRU_WIDTH), F32),
    ]
    h_scratch = (1, SSD_GROUPS, SSD_STATE, SSD_GWIDTH) if NC > 1 else (1, 1, SUBLANES, LANES)
    scratch = [
        pltpu.VMEM((S, POOL_TILE + T, POOL_WIDTH), F32),
        pltpu.VMEM((S, CONV_TILE, SSD_CONV_DIM), F32),
        pltpu.VMEM((S, CONV_TILE, LRU_WIDTH), F32),
        pltpu.VMEM((S, SUBLANES, LRU_WIDTH), F32),
        pltpu.VMEM(h_scratch, F32),
    ]
    if fused:
        scratch += [pltpu.VMEM((R, N_PROJ), F32), pltpu.VMEM((R, N_PROJ), F32), pltpu.VMEM((R, D_MIX), BF16)]
    return pl.pallas_call(
        functools.partial(_mix_kernel, S=S, T=T, NC=NC, LV=LV, pos0=pos0, fused=fused),
        grid=(nseq // S, NC),
        in_specs=in_specs,
        out_specs=out_specs,
        out_shape=out_shape,
        scratch_shapes=scratch,
        input_output_aliases=aliases,
        name="mix_seq%d" % S,
        compiler_params=_cparams(("arbitrary" if fused else "parallel", "arbitrary")),
    )(*lead_args, st[0], st[1], ssd_state, st[2], st[3], *w, ssd_new)


def _expand_matrix():
    e = np.zeros((LANES, SSD_WIDTH), np.float32)
    for k in range(SSD_HEADS):
        e[k, k * SSD_HEAD_DIM:(k + 1) * SSD_HEAD_DIM] = 1.0
    return jnp.asarray(np.concatenate([e, e], axis=0), BF16)


def _pair_block_diag(w):
    pairs = w.reshape(LRU_HEADS // 2, 2, LRU_HDIM, LRU_HDIM)
    eye = jnp.eye(2, dtype=w.dtype)
    return jnp.einsum('phij,hg->phigj', pairs, eye).reshape(LRU_HEADS // 2, 2 * LRU_HDIM, 2 * LRU_HDIM)


def _pad_lanes(v):
    return jnp.pad(v, (0, LANES - v.shape[0])).reshape(1, LANES)


def _matmul_weights(P):
    named = dict(w_in=P['w_in'], w_out=P['w_out'], wq=P['w_mem_q'], wo=P['w_mem_o'], wg=P['w_ffn_gate'],
                 wu=P['w_ffn_up'], wd=P['w_ffn_down'])
    big = {k: v.astype(BF16) for k, v in named.items()}
    w_in = big['w_in']
    big['w_in_tail'] = jnp.concatenate(
        [w_in[:, :, C_GATE + SSD_HEADS:], w_in[:, :, C_GATE:C_GATE + SSD_HEADS],
         jnp.zeros((DEPTH, D_MODEL, N_PROJ - C_DT - SSD_HEADS), BF16)], axis=2)
    return big


def _layer_params(l, P, big):
    mix_w = (
        P['pool_w'][l].astype(BF16),
        P['pool_scale'][l].reshape(1, POOL_WIDTH),
        P['ssd_conv_w'][l],
        P['ssd_conv_b'][l].reshape(1, SSD_CONV_DIM),
        _pad_lanes(P['ssd_dt_bias'][l]),
        _pad_lanes(P['ssd_a_log'][l]),
        jnp.repeat(P['ssd_d'][l], SSD_HEAD_DIM).reshape(1, SSD_WIDTH),
        P['ssd_norm'][l].reshape(1, SSD_WIDTH),
        P['lru_conv_w'][l],
        P['lru_conv_b'][l].reshape(1, LRU_WIDTH),
        jnp.concatenate([_pair_block_diag(P['lru_wa'][l]), _pair_block_diag(P['lru_wx'][l])], axis=2).astype(BF16),
        jnp.concatenate([P['lru_ba'][l].reshape(1, LRU_WIDTH), P['lru_bx'][l].reshape(1, LRU_WIDTH)], axis=1),
        P['lru_lambda'][l].reshape(1, LRU_WIDTH),
        _expand_matrix(),
    )
    out = dict(mix=mix_w, norm_mix=P['norm_mix'][l], norm_mem=P['norm_mem'][l], norm_ffn=P['norm_ffn'][l])
    out.update({k: (v, l) for k, v in big.items()})
    return out


def _run_trunk(h, layers, norm_final, states, ssd_state, attend, *, mix_cfg, fuse_proj):
    new_states = []
    ssd_new = None
    for l, lp in enumerate(layers):
        ssd_layer = min(l, ssd_state.shape[0] - 1)
        if fuse_proj:
            h, *st = _mix(h, states[l], ssd_state, ssd_layer, ssd_new, l, lp['mix'],
                          proj=(lp['norm_mix'], lp['w_in'], lp['w_in_tail'], lp['w_out']), **mix_cfg)
        else:
            proj = _linear(h, lp['w_in'], name="in_proj", tail=lp['w_in_tail'], gain=lp['norm_mix'])
            ycat, *st = _mix(proj, states[l], ssd_state, ssd_layer, ssd_new, l, lp['mix'], **mix_cfg)
            h = _linear(ycat, lp['w_out'], name="out_proj", residual=h)
        ssd_new = st[2]
        h = attend(h, lp, l)
        last = l == len(layers) - 1
        h = _ffn(h, lp['norm_ffn'], lp['wg'], lp['wu'], lp['wd'], final_gain=norm_final if last else None)
        new_states.append(st)
    return h, new_states


def _unpack_states(new_states, nseq):
    pool = jnp.stack([s[0][:, 1:POOL_TILE] for s in new_states])
    sconv = jnp.stack([s[1][:, CONV_TILE - 3:CONV_TILE] for s in new_states])
    ssd = new_states[-1][2].reshape(DEPTH, nseq, SSD_HEADS, SSD_HEAD_DIM, SSD_STATE)
    lconv = jnp.stack([s[3][:, CONV_TILE - 3:CONV_TILE] for s in new_states])
    lru = jnp.stack([s[4][:, 0] for s in new_states])
    return pool, sconv, ssd, lconv, lru


def _trunk_prompt(x_prompt, mem_prompt, P, layers, *, chunk=256, tq=1024):
    batch, seq, _ = x_prompt.shape
    chunk = min(chunk, seq)
    tq = min(tq, seq)
    wk = P['w_mem_k'].astype(BF16)
    wv = P['w_mem_v'].astype(BF16)
    mem_k, mem_v = _kv_proj(mem_prompt.reshape(batch * N_MEM, D_MODEL), wk, wv)
    mem_k = mem_k.reshape(DEPTH, batch, N_MEM, D_MODEL)
    mem_v = mem_v.reshape(DEPTH, batch, N_MEM, D_MODEL)
    zeros = (jnp.zeros((batch, POOL_TILE, POOL_WIDTH), F32), jnp.zeros((batch, CONV_TILE, SSD_CONV_DIM), F32),
             jnp.zeros((batch, CONV_TILE, LRU_WIDTH), F32), jnp.zeros((batch, SUBLANES, LRU_WIDTH), F32))
    ssd0 = jnp.zeros((1, batch, SSD_WIDTH, SSD_STATE), F32)
    y, new_states = _run_trunk(
        x_prompt.reshape(batch * seq, D_MODEL), layers, P['norm_final'], [zeros] * DEPTH, ssd0,
        lambda h, lp, l: _attn_block(h, lp['norm_mem'], lp['wq'], lp['wo'], mem_k, mem_v, l, tq=tq, seq_rows=seq),
        mix_cfg=dict(S=1, T=chunk, NC=seq // chunk, LV=chunk, pos0=0), fuse_proj=True)
    return (y.reshape(batch, seq, D_MODEL),) + _unpack_states(new_states, batch) + (
        mem_k.reshape(DEPTH, batch, N_MEM, MEM_HEADS, MEM_HDIM),
        mem_v.reshape(DEPTH, batch, N_MEM, MEM_HEADS, MEM_HDIM))


def _trunk_sample(x_sample, past_len, state_pool, state_ssd_conv, state_ssd, state_lru_conv, state_lru,
                  cache_k, cache_v, P, layers, *, seq_blk=16, attn_blk=8):
    nseq, seq, _ = x_sample.shape
    assert seq <= SUBLANES
    pad_t = SUBLANES - seq
    h = jnp.pad(x_sample, ((0, 0), (0, pad_t), (0, 0))).reshape(nseq * SUBLANES, D_MODEL)
    states = []
    for l in range(DEPTH):
        states.append((
            jnp.pad(state_pool[l], ((0, 0), (POOL_TILE - POOL_BUF, 0), (0, 0))),
            jnp.pad(state_ssd_conv[l], ((0, 0), (CONV_TILE - 3, 0), (0, 0))),
            jnp.pad(state_lru_conv[l], ((0, 0), (CONV_TILE - 3, 0), (0, 0))),
            jnp.broadcast_to(state_lru[l][:, None, :], (nseq, SUBLANES, LRU_WIDTH)),
        ))
    kc, vc = _cache_view(cache_k), _cache_view(cache_v)

    def attend(h, lp, l):
        q = _linear(h, lp['wq'], name="q_proj", gain=lp['norm_mem'])
        return _linear(_cache_attn(q, kc, vc, l, nseq_blk=attn_blk), lp['wo'], name="o_proj", residual=h)

    y, new_states = _run_trunk(
        h, layers, P['norm_final'], states, state_ssd.reshape(DEPTH, nseq, SSD_WIDTH, SSD_STATE), attend,
        mix_cfg=dict(S=seq_blk, T=SUBLANES, NC=1, LV=seq, pos0=past_len), fuse_proj=False)
    y = y.reshape(nseq, SUBLANES, D_MODEL)[:, :seq]
    return (y,) + _unpack_states(new_states, nseq)


def kernel(x_prompt, x_sample, mem_prompt, state_pool, state_ssd_conv, state_ssd, state_lru_conv, state_lru,
           cache_mem_k, cache_mem_v, norm_mix, w_in, pool_w, pool_scale, ssd_conv_w, ssd_conv_b, ssd_dt_bias,
           ssd_a_log, ssd_d, ssd_norm, lru_conv_w, lru_conv_b, lru_wa, lru_ba, lru_wx, lru_bx, lru_lambda,
           w_out, norm_mem, w_mem_q, w_mem_k, w_mem_v, w_mem_o, norm_ffn, w_ffn_gate, w_ffn_up, w_ffn_down,
           norm_final):
    P = dict(norm_mix=norm_mix, w_in=w_in, pool_w=pool_w, pool_scale=pool_scale, ssd_conv_w=ssd_conv_w,
             ssd_conv_b=ssd_conv_b, ssd_dt_bias=ssd_dt_bias, ssd_a_log=ssd_a_log, ssd_d=ssd_d, ssd_norm=ssd_norm,
             lru_conv_w=lru_conv_w, lru_conv_b=lru_conv_b, lru_wa=lru_wa, lru_ba=lru_ba, lru_wx=lru_wx,
             lru_bx=lru_bx, lru_lambda=lru_lambda, w_out=w_out, norm_mem=norm_mem, w_mem_q=w_mem_q,
             w_mem_k=w_mem_k, w_mem_v=w_mem_v, w_mem_o=w_mem_o, norm_ffn=norm_ffn, w_ffn_gate=w_ffn_gate,
             w_ffn_up=w_ffn_up, w_ffn_down=w_ffn_down, norm_final=norm_final)
    big = _matmul_weights(P)
    layers = [_layer_params(l, P, big) for l in range(DEPTH)]
    (y_prompt, p_pool, p_sconv, p_ssd, p_lconv, p_lru, p_mem_k, p_mem_v) = _trunk_prompt(
        x_prompt, mem_prompt, P, layers)
    (y_sample, s_pool, s_sconv, s_ssd, s_lconv, s_lru) = _trunk_sample(
        x_sample, PAST_LEN, state_pool, state_ssd_conv, state_ssd, state_lru_conv, state_lru,
        cache_mem_k, cache_mem_v, P, layers)
    return (y_prompt, y_sample, p_pool, p_sconv, p_ssd, p_lconv, p_lru, p_mem_k, p_mem_v,
            s_pool, s_sconv, s_ssd, s_lconv, s_lru)
```
